```python
import jax
import jax.numpy as jnp
from jax import lax
import numpy as np

D_MODEL = 1024
BATCH = 8
SEQ = 2048
DEPTH = 1

EPS = 1e-6
HG_HEADS = 8
HG_DK = 128
HG_DV = D_MODEL // HG_HEADS
HG_KW = HG_HEADS * HG_DK
HG_VW = HG_HEADS * HG_DV
HG_CHUNK = 32
CONV_WIDTH = D_MODEL
CONV_K = 31
IN_SIZES = (HG_KW, HG_KW, HG_VW, HG_VW, CONV_WIDTH, CONV_WIDTH, D_MODEL, D_MODEL)
IN_SPLITS = tuple(int(s) for s in np.cumsum(IN_SIZES)[:-1])
N_IN = sum(IN_SIZES)
PEER_HEADS = 8
PEER_NKEYS = 128
PEER_EXPERTS = PEER_NKEYS * PEER_NKEYS
PEER_DQ = 256
PEER_TOPK = 16
PEER_BLOCK = 128

kernel_name = 'hybrid_hgrn2_conformer_peer_adaln'


def rms_norm(x, g):
    xf = x.astype(jnp.float32)
    y = xf * lax.rsqrt(jnp.mean(xf * xf, axis=-1, keepdims=True) + EPS)
    return (y * g.astype(jnp.float32)).astype(x.dtype)


def layer_norm(x, g, b):
    xf = x.astype(jnp.float32)
    mu = jnp.mean(xf, axis=-1, keepdims=True)
    xc = xf - mu
    y = xc * lax.rsqrt(jnp.mean(xc * xc, axis=-1, keepdims=True) + EPS)
    return (y * g.astype(jnp.float32) + b.astype(jnp.float32)).astype(x.dtype)


def hgrn2_chunkwise(q, k, v, logf):
    B, S, H, dk = q.shape
    dv = v.shape[-1]
    n = S // HG_CHUNK

    def to_chunks(t):
        return t.reshape(B, n, HG_CHUNK, H, t.shape[-1]).transpose(1, 0, 3, 2, 4)

    causal = jnp.tril(jnp.ones((HG_CHUNK, HG_CHUNK), dtype=bool))

    def step(state, inp):
        qt, kt, vt, lt = inp
        bcum = jnp.cumsum(lt, axis=2)
        rel = bcum[:, :, :, None, :] - bcum[:, :, None, :, :]
        decay = jnp.exp(jnp.where(causal[:, :, None], rel, -jnp.inf))
        scores = jnp.einsum('bhtk,bhsk,bhtsk->bhts', qt, kt, decay)
        o = jnp.einsum('bhts,bhsv->bhtv', scores, vt)
        o = o + jnp.einsum('bhtk,bhkv->bhtv', qt * jnp.exp(bcum), state)
        blast = bcum[:, :, -1:, :]
        state = (jnp.exp(blast[:, :, 0, :])[..., None] * state
                 + jnp.einsum('bhsk,bhsv->bhkv', kt * jnp.exp(blast - bcum), vt))
        return state, o

    s0 = jnp.zeros((B, H, dk, dv), jnp.float32)
    _, o = lax.scan(step, s0, (to_chunks(q), to_chunks(k), to_chunks(v), to_chunks(logf)))
    return o.transpose(1, 0, 3, 2, 4).reshape(B, S, H, dv)


def causal_depthwise_conv(u, w, b):
    y = lax.conv_general_dilated(
        u, w[:, None, :].astype(u.dtype), window_strides=(1,),
        padding=((CONV_K - 1, 0),), dimension_numbers=('NWC', 'WIO', 'NWC'),
        feature_group_count=u.shape[-1])
    return y + b


def peer_layer(h, w_q, sub_keys, u_tab, v_tab):
    B, S, D = h.shape
    T = B * S
    hf = h.reshape(T, D)
    q = (hf @ w_q).reshape(T, PEER_HEADS, 2, PEER_DQ // 2)
    s = jnp.einsum('thpd,hpnd->thpn', q, sub_keys)
    sc, idx = lax.top_k(s, PEER_TOPK)
    cand = (sc[:, :, 0, :, None] + sc[:, :, 1, None, :]).reshape(T, PEER_HEADS, PEER_TOPK * PEER_TOPK)
    cidx = (idx[:, :, 0, :, None] * PEER_NKEYS + idx[:, :, 1, None, :]).reshape(T, PEER_HEADS, PEER_TOPK * PEER_TOPK)
    top_sc, pos = lax.top_k(cand, PEER_TOPK)
    eidx = jnp.take_along_axis(cidx, pos, axis=-1)
    gate = jax.nn.softmax(top_sc.astype(jnp.float32), axis=-1).astype(h.dtype)
    nb = T // PEER_BLOCK

    def block(args):
        hb, eb, gb = args
        act = jax.nn.gelu(jnp.einsum('thkd,td->thk', u_tab[eb], hb), approximate=False)
        return jnp.einsum('thk,thkd->td', gb * act, v_tab[eb])

    y = lax.map(block, (hf.reshape(nb, PEER_BLOCK, D),
                        eidx.reshape(nb, PEER_BLOCK, PEER_HEADS, PEER_TOPK),
                        gate.reshape(nb, PEER_BLOCK, PEER_HEADS, PEER_TOPK)))
    return y.reshape(B, S, D)


def setup_inputs(seed: int = 0) -> dict:
    key = jax.random.key(seed)
    ks = jax.random.split(key, 24)
    L = DEPTH

    def nrm(k, shape, scale):
        return jax.random.normal(k, shape, jnp.float32) * scale

    return {
        'x': nrm(ks[0], (BATCH, SEQ, D_MODEL), 1.0),
        'c': nrm(ks[1], (BATCH, D_MODEL), 1.0),
        'ada_w': nrm(ks[2], (L, D_MODEL, 6 * D_MODEL), 0.5 * D_MODEL ** -0.5),
        'ada_b': nrm(ks[3], (L, 6 * D_MODEL), 0.02),
        'norm1_g': 1.0 + nrm(ks[4], (L, D_MODEL), 0.02),
        'w_in': nrm(ks[5], (L, D_MODEL, N_IN), D_MODEL ** -0.5),
        'lb_logits': nrm(ks[6], (L + 1, HG_KW), 0.5),
        'hg_norm_g': 1.0 + nrm(ks[7], (L, HG_VW), 0.02),
        'w_a': nrm(ks[8], (L, HG_VW, D_MODEL), HG_VW ** -0.5),
        'conv_w': nrm(ks[9], (L, CONV_K, CONV_WIDTH), CONV_K ** -0.5),
        'conv_b': nrm(ks[10], (L, CONV_WIDTH), 0.02),
        'conv_ln_g': 1.0 + nrm(ks[11], (L, CONV_WIDTH), 0.02),
        'conv_ln_b': nrm(ks[12], (L, CONV_WIDTH), 0.02),
        'w_b': nrm(ks[13], (L, CONV_WIDTH, D_MODEL), CONV_WIDTH ** -0.5),
        'w_out': nrm(ks[14], (L, D_MODEL, D_MODEL), D_MODEL ** -0.5),
        'norm2_g': 1.0 + nrm(ks[15], (L, D_MODEL), 0.02),
        'peer_wq': nrm(ks[16], (L, D_MODEL, PEER_HEADS * PEER_DQ), D_MODEL ** -0.5),
        'peer_keys': nrm(ks[17], (L, PEER_HEADS, 2, PEER_NKEYS, PEER_DQ // 2), (PEER_DQ // 2) ** -0.5),
        'peer_u': nrm(ks[18], (L, PEER_EXPERTS, D_MODEL), D_MODEL ** -0.5),
        'peer_v': nrm(ks[19], (L, PEER_EXPERTS, D_MODEL), PEER_HEADS ** -0.5),
        'final_g': 1.0 + nrm(ks[20], (D_MODEL,), 0.02),
    }


def reference(x, c, ada_w, ada_b, norm1_g, w_in, lb_logits, hg_norm_g, w_a, conv_w, conv_b,
              conv_ln_g, conv_ln_b, w_b, w_out, norm2_g, peer_wq, peer_keys, peer_u, peer_v,
              final_g):
    B, S, D = x.shape
    lower_bounds = jnp.cumsum(jax.nn.softmax(lb_logits.astype(jnp.float32), axis=0), axis=0)

    def heads(t):
        return t.reshape(B, S, HG_HEADS, -1)

    h = x
    for l in range(DEPTH):
        ada = (jax.nn.silu(c) @ ada_w[l] + ada_b[l])[:, None, :]
        sh1, sc1, g1, sh2, sc2, g2 = jnp.split(ada, 6, axis=-1)

        u = rms_norm(h, norm1_g[l]) * (1 + sc1) + sh1
        z = u @ w_in[l]
        zq, zf, zi, zg, zca, zcb, zga, zgb = jnp.split(z, IN_SPLITS, axis=-1)

        lb = lower_bounds[l]
        f = lb + (1.0 - lb) * jax.nn.sigmoid(zf.astype(jnp.float32))
        o = hgrn2_chunkwise(heads(zq.astype(jnp.float32)), heads(1.0 - f),
                            heads(zi.astype(jnp.float32)), heads(jnp.log(f)))
        o = rms_norm(o, hg_norm_g[l].reshape(HG_HEADS, HG_DV)).reshape(B, S, HG_VW).astype(x.dtype)
        y_a = (o * jax.nn.silu(zg)) @ w_a[l]

        glu = zca * jax.nn.sigmoid(zcb)
        cv = causal_depthwise_conv(glu, conv_w[l], conv_b[l])
        cv = jax.nn.silu(layer_norm(cv, conv_ln_g[l], conv_ln_b[l]))
        y_b = cv @ w_b[l]

        merged = jax.nn.sigmoid(zga) * y_a + jax.nn.sigmoid(zgb) * y_b
        h = h + g1 * (merged @ w_out[l])

        u2 = rms_norm(h, norm2_g[l]) * (1 + sc2) + sh2
        h = h + g2 * peer_layer(u2, peer_wq[l], peer_keys[l], peer_u[l], peer_v[l])

    return rms_norm(h, final_g)
```

```python
import functools

import jax
import jax.numpy as jnp
from jax import lax
from jax.experimental import pallas as pl
from jax.experimental.pallas import tpu as pltpu

F32 = jnp.float32
BF16 = jnp.bfloat16
EPS = 1e-6
LANES = 128
SUBLANES = 8
HG_HEADS = 8
HG_CHUNK = 128
CONV_K = 31
CONV_HALO = 32
PEER_HEADS = 8
PEER_TOPK = 16
VMEM_LIMIT = 56 * 1024 * 1024

NT_DIMS = (((1,), (1,)), ((), ()))
TN_DIMS = (((0,), (0,)), ((), ()))


def _dot(a, b):
    return jnp.dot(a, b, preferred_element_type=F32)


def _dot_nt(a, b):
    return lax.dot_general(a, b, NT_DIMS, preferred_element_type=F32)


def _rms(x, g):
    ms = jnp.mean(x * x, axis=-1, keepdims=True)
    return x * lax.rsqrt(ms + EPS) * g


def _sigmoid(x):
    return 1.0 / (1.0 + jnp.exp(-x))


def _gelu(x):
    return 0.5 * x * (1.0 + lax.erf(x * (0.5 ** 0.5)))


def _params(sem):
    return pltpu.CompilerParams(dimension_semantics=sem, vmem_limit_bytes=VMEM_LIMIT)


def _ada_kernel(c_ref, w_ref, b_ref, o_ref):
    c = c_ref[...]
    s = (c * _sigmoid(c)).astype(BF16)
    o_ref[...] = _dot(s, w_ref[...]) + b_ref[...]


def _ada(c, ada_w, ada_b):
    bsz, d = c.shape
    n = ada_w.shape[1]
    return pl.pallas_call(
        _ada_kernel,
        grid=(n // d,),
        in_specs=[pl.BlockSpec((bsz, d), lambda j: (0, 0)),
                  pl.BlockSpec((d, d), lambda j: (0, j)),
                  pl.BlockSpec((1, d), lambda j: (0, j))],
        out_specs=pl.BlockSpec((bsz, d), lambda j: (0, j)),
        out_shape=jax.ShapeDtypeStruct((bsz, n), F32),
        compiler_params=_params(("arbitrary",)),
        name="ada",
    )(c, ada_w, ada_b)


def _in_proj_kernel(x_ref, mod_ref, g_ref, lbl_ref, w_ref,
                    q_o, f_o, v_o, sg_o, glu_o, ga_o, gb_o, u_s, zca_s):
    j = pl.program_id(1)

    @pl.when(j == 0)
    def _():
        m = mod_ref[0]
        u = _rms(x_ref[...], g_ref[...]) * (1.0 + m[1:2]) + m[0:1]
        u_s[...] = u.astype(BF16)

    z = _dot(u_s[...], w_ref[...])

    @pl.when(j == 0)
    def _():
        q_o[...] = z.astype(BF16)

    @pl.when(j == 1)
    def _():
        lg = lbl_ref[...]
        e = jnp.exp(lg - jnp.max(lg, axis=0, keepdims=True))
        lb = e[0:1] / jnp.sum(e, axis=0, keepdims=True)
        f_o[...] = lb + (1.0 - lb) * _sigmoid(z)

    @pl.when(j == 2)
    def _():
        v_o[...] = z.astype(BF16)

    @pl.when(j == 3)
    def _():
        sg_o[...] = (z * _sigmoid(z)).astype(BF16)

    @pl.when(j == 4)
    def _():
        zca_s[...] = z

    @pl.when(j == 5)
    def _():
        glu_o[...] = zca_s[...] * _sigmoid(z)

    @pl.when(j == 6)
    def _():
        ga_o[...] = _sigmoid(z).astype(BF16)

    @pl.when(j == 7)
    def _():
        gb_o[...] = _sigmoid(z).astype(BF16)


def _in_proj(x2, mod, g, lb_logits, w_in, seq, tm):
    t, d = x2.shape
    nj = w_in.shape[1] // d
    tpb = seq // tm
    row = lambda i, j: (i, 0)
    out_dt = (BF16, F32, BF16, BF16, F32, BF16, BF16)
    return pl.pallas_call(
        _in_proj_kernel,
        grid=(t // tm, nj),
        in_specs=[pl.BlockSpec((tm, d), row),
                  pl.BlockSpec((1,) + mod.shape[1:], lambda i, j: (i // tpb, 0, 0)),
                  pl.BlockSpec((1, d), lambda i, j: (0, 0)),
                  pl.BlockSpec(lb_logits.shape, lambda i, j: (0, 0)),
                  pl.BlockSpec((d, d), lambda i, j: (0, j))],
        out_specs=[pl.BlockSpec((tm, d), row) for _ in out_dt],
        out_shape=[jax.ShapeDtypeStruct((t, d), dt) for dt in out_dt],
        scratch_shapes=[pltpu.VMEM((tm, d), BF16), pltpu.VMEM((tm, d), F32)],
        compiler_params=_params(("arbitrary", "arbitrary")),
        name="in_proj",
    )(x2, mod, g, lb_logits, w_in)


def _split3(x):
    hi = x.astype(BF16)
    r = x - hi.astype(F32)
    mid = r.astype(BF16)
    lo = (r - mid.astype(F32)).astype(BF16)
    return hi, mid, lo


def _hgrn2_kernel(q_ref, f_ref, v_ref, sg_ref, g_ref, o_ref, *, seq):
    c = HG_CHUNK
    rows = lax.broadcasted_iota(jnp.int32, (c, c), 0)
    cols = lax.broadcasted_iota(jnp.int32, (c, c), 1)
    causal = rows >= cols
    ones_tri = jnp.where(causal, 1.0, 0.0).astype(BF16)
    g = g_ref[...]

    def chunk(n, st):
        sl = pl.ds(pl.multiple_of(n * c, c), c)
        f = f_ref[sl, :]
        hi, mid, lo = _split3(jnp.log(f))
        b = _dot(ones_tri, hi) + _dot(ones_tri, mid) + _dot(ones_tri, lo)
        b_mid = b[c // 2 - 1:c // 2, :]
        b_last = b[c - 1:c, :]
        q = q_ref[sl, :].astype(F32)
        k = 1.0 - f
        v = v_ref[sl, :]
        qd = (q * jnp.exp(b - b_mid)).astype(BF16)
        kd = (k * jnp.exp(b_mid - b)).astype(BF16)
        scores = jnp.where(causal, _dot_nt(qd, kd), 0.0).astype(BF16)
        o = _dot(scores, v)
        qe = (q * jnp.exp(b)).astype(BF16)
        o = o + _dot_nt(qe, st.astype(BF16))
        kl = (k * jnp.exp(b_last - b)).astype(BF16)
        st = st * jnp.exp(b_last) + lax.dot_general(v, kl, TN_DIMS, preferred_element_type=F32)
        on = _rms(o, g)
        o_ref[sl, :] = (on * sg_ref[sl, :].astype(F32)).astype(BF16)
        return st

    dk = q_ref.shape[1]
    dv = v_ref.shape[1]
    lax.fori_loop(0, seq // c, chunk, jnp.zeros((dv, dk), F32))


def _hgrn2(q, f, v, sg, g, bsz, seq):
    t, d = q.shape
    hd = d // HG_HEADS
    blk = pl.BlockSpec((seq, hd), lambda b, h: (b, h))
    return pl.pallas_call(
        functools.partial(_hgrn2_kernel, seq=seq),
        grid=(bsz, HG_HEADS),
        in_specs=[blk, blk, blk, blk, pl.BlockSpec((1, hd), lambda b, h: (0, h))],
        out_specs=blk,
        out_shape=jax.ShapeDtypeStruct((t, d), BF16),
        compiler_params=_params(("arbitrary", "arbitrary")),
        name="hgrn2",
    )(q, f, v, sg, g)


def _conv_kernel(x_ref, w_ref, b_ref, lg_ref, lb_ref, o_ref, xbuf, ybuf, *, ts, rb):
    s = pl.program_id(1)
    halo = CONV_HALO

    @pl.when(s == 0)
    def _():
        xbuf[0:halo, :] = jnp.zeros((halo, xbuf.shape[1]), F32)

    @pl.when(s > 0)
    def _():
        xbuf[0:halo, :] = xbuf[ts:ts + halo, :]

    xbuf[halo:halo + ts, :] = x_ref[...]

    off = halo - (CONV_K - 1)
    d = xbuf.shape[1]
    for r0 in range(0, ts, rb):
        for c0 in range(0, d, LANES):
            acc = jnp.zeros((rb, LANES), F32)
            for j in range(CONV_K):
                xs = xbuf[r0 + off + j:r0 + off + j + rb, c0:c0 + LANES]
                acc = acc + xs * w_ref[j:j + 1, c0:c0 + LANES]
            ybuf[r0:r0 + rb, c0:c0 + LANES] = acc

    y = ybuf[...] + b_ref[...]
    mu = jnp.mean(y, axis=-1, keepdims=True)
    yc = y - mu
    var = jnp.mean(yc * yc, axis=-1, keepdims=True)
    z = yc * lax.rsqrt(var + EPS) * lg_ref[...] + lb_ref[...]
    o_ref[...] = (z * _sigmoid(z)).astype(BF16)


def _conv(glu, w, b, lg, lb, bsz, seq, ts, rb):
    t, d = glu.shape
    ns = seq // ts
    vec = pl.BlockSpec((1, d), lambda bi, s: (0, 0))
    return pl.pallas_call(
        functools.partial(_conv_kernel, ts=ts, rb=rb),
        grid=(bsz, ns),
        in_specs=[pl.BlockSpec((ts, d), lambda bi, s: (bi * ns + s, 0)),
                  pl.BlockSpec(w.shape, lambda bi, s: (0, 0)),
                  vec, vec, vec],
        out_specs=pl.BlockSpec((ts, d), lambda bi, s: (bi * ns + s, 0)),
        out_shape=jax.ShapeDtypeStruct((t, d), BF16),
        scratch_shapes=[pltpu.VMEM((ts + CONV_HALO, d), F32), pltpu.VMEM((ts, d), F32)],
        compiler_params=_params(("arbitrary", "arbitrary")),
        name="conv",
    )(glu, w, b, lg, lb)


def _merge_kernel(og_ref, cv_ref, ga_ref, gb_ref, x_ref, mod_ref, wa_ref, wb_ref, wo_ref, h_ref):
    ya = _dot(og_ref[...], wa_ref[...])
    yb = _dot(cv_ref[...], wb_ref[...])
    merged = ga_ref[...].astype(F32) * ya + gb_ref[...].astype(F32) * yb
    m = mod_ref[0]
    h_ref[...] = x_ref[...] + m[2:3] * _dot(merged.astype(BF16), wo_ref[...])


def _merge(og, cv, ga, gb, x2, mod, w_a, w_b, w_out, seq, tm):
    t, d = x2.shape
    tpb = seq // tm
    row = pl.BlockSpec((tm, d), lambda i: (i, 0))
    wsp = pl.BlockSpec((d, d), lambda i: (0, 0))
    return pl.pallas_call(
        _merge_kernel,
        grid=(t // tm,),
        in_specs=[row, row, row, row, row,
                  pl.BlockSpec((1,) + mod.shape[1:], lambda i: (i // tpb, 0, 0)),
                  wsp, wsp, wsp],
        out_specs=row,
        out_shape=jax.ShapeDtypeStruct((t, d), F32),
        compiler_params=_params(("arbitrary",)),
        name="merge",
    )(og, cv, ga, gb, x2, mod, w_a, w_b, w_out)


def _top_values(x, k):
    out = []
    for _ in range(k):
        m = jnp.max(x, axis=0, keepdims=True)
        out.append(m)
        x = jnp.where(x == m, -jnp.inf, x)
    return out


def _peer_prep(h_ref, mod_ref, g_ref, wq_ref, keys_ref, u2_s, s_s, e2_s, th_s, e1_s, top_s, pk_s, tt):
    nh = PEER_HEADS
    kk = PEER_TOPK
    nk = keys_ref.shape[1]
    dq = keys_ref.shape[2]
    m = mod_ref[0]
    u2 = _rms(h_ref[...], g_ref[...]) * (1.0 + m[4:5]) + m[3:4]
    u2_s[...] = u2.astype(BF16)
    q = _dot(u2_s[...], wq_ref[...]).astype(BF16)
    for a in range(2 * nh):
        s_s[a] = _dot_nt(keys_ref[a], q[:, a * dq:(a + 1) * dq])

    def per_group(gi, carry):
        ln = pl.ds(pl.multiple_of(gi * LANES, LANES), LANES)
        for a in range(2 * nh):
            vals = _top_values(s_s[a, :, ln], kk)
            for r in range(kk):
                top_s[a % 2, r, a // 2:a // 2 + 1, ln] = vals[r]
        ta = [top_s[0, r, :, ln] for r in range(kk)]
        tb = [top_s[1, r, :, ln] for r in range(kk)]
        cand = [ta[i] + tb[j] for i in range(kk) for j in range(kk // (i + 1))]
        best = []
        for _ in range(kk):
            mx = cand[0]
            for cv in cand[1:]:
                mx = jnp.maximum(mx, cv)
            best.append(mx)
            cand = [jnp.where(cv == mx, -jnp.inf, cv) for cv in cand]
        c0 = best[0]
        z = jnp.ones_like(c0)
        for cv in best[1:]:
            z = z + jnp.exp(cv - c0)
        pk_s[0, :, ln] = best[kk - 1]
        pk_s[1, :, ln] = ta[0]
        pk_s[2, :, ln] = tb[0]
        pk_s[3, :, ln] = 1.0 / z
        return carry

    lax.fori_loop(0, tt // LANES, per_group, 0)

    ng = nk // SUBLANES
    for hh in range(nh):
        s2 = s_s[2 * hh + 1]
        e2_s[hh] = jnp.exp(s2 - pk_s[2, hh:hh + 1, :])
        for gi in range(ng):
            s1 = s_s[2 * hh, gi * SUBLANES:(gi + 1) * SUBLANES, :]
            th_s[hh * ng + gi] = pk_s[0, hh:hh + 1, :] - s1
            e1_s[hh * ng + gi] = jnp.exp(s1 - pk_s[1, hh:hh + 1, :]) * pk_s[3, hh:hh + 1, :]


def _peer_kernel(h_ref, mod_ref, g2_ref, gf_ref, wq_ref, keys_ref, u_ref, vt_ref, o_ref,
                 u2_s, s_s, e2_s, th_s, e1_s, top_s, pk_s, a_s, hm_s, acc_s, *, tt, eb, rb):
    j = pl.program_id(1)
    nh = PEER_HEADS
    nk = keys_ref.shape[1]
    nrow = eb // nk
    assert nrow == SUBLANES
    ng = nk // SUBLANES

    @pl.when(j == 0)
    def _():
        _peer_prep(h_ref, mod_ref, g2_ref, wq_ref, keys_ref, u2_s, s_s, e2_s, th_s, e1_s, top_s, pk_s, tt)
        acc_s[...] = jnp.zeros_like(acc_s)

    a_s[...] = _dot_nt(u_ref[...], u2_s[...])

    for g0 in range(0, tt, LANES):
        for r0 in range(0, nk, rb):
            gate = [jnp.zeros((rb, LANES), F32) for _ in range(nrow)]
            for hh in range(nh):
                s2 = s_s[2 * hh + 1, r0:r0 + rb, g0:g0 + LANES]
                e2 = e2_s[hh, r0:r0 + rb, g0:g0 + LANES]
                for il in range(nrow):
                    th = th_s[hh * ng + j, il:il + 1, g0:g0 + LANES]
                    e1 = e1_s[hh * ng + j, il:il + 1, g0:g0 + LANES]
                    gate[il] = gate[il] + jnp.where(s2 >= th, e2, 0.0) * e1
            for il in range(nrow):
                a = a_s[il * nk + r0:il * nk + r0 + rb, g0:g0 + LANES]
                act = _gelu(a)
                hm_s[il * nk + r0:il * nk + r0 + rb, g0:g0 + LANES] = (act * gate[il]).astype(BF16)

    acc_s[...] += _dot(vt_ref[...], hm_s[...])

    @pl.when(j == pl.num_programs(1) - 1)
    def _():
        m = mod_ref[0]
        hn = h_ref[...] + m[5:6] * acc_s[...].T
        o_ref[...] = _rms(hn, gf_ref[...])


def _peer(h1, mod, g2, gf, wq, keys, u_tab, vt_tab, seq, tt, eb, rb):
    t, d = h1.shape
    ne = u_tab.shape[0]
    nk = keys.shape[1]
    tpb = seq // tt
    vec = pl.BlockSpec((1, d), lambda i, j: (0, 0))
    return pl.pallas_call(
        functools.partial(_peer_kernel, tt=tt, eb=eb, rb=rb),
        grid=(t // tt, ne // eb),
        in_specs=[pl.BlockSpec((tt, d), lambda i, j: (i, 0)),
                  pl.BlockSpec((1,) + mod.shape[1:], lambda i, j: (i // tpb, 0, 0)),
                  vec, vec,
                  pl.BlockSpec(wq.shape, lambda i, j: (0, 0)),
                  pl.BlockSpec(keys.shape, lambda i, j: (0, 0, 0)),
                  pl.BlockSpec((eb, d), lambda i, j: (j, 0)),
                  pl.BlockSpec((d, eb), lambda i, j: (0, j))],
        out_specs=pl.BlockSpec((tt, d), lambda i, j: (i, 0)),
        out_shape=jax.ShapeDtypeStruct((t, d), F32),
        scratch_shapes=[pltpu.VMEM((tt, d), BF16),
                        pltpu.VMEM((2 * PEER_HEADS, nk, tt), F32),
                        pltpu.VMEM((PEER_HEADS, nk, tt), F32),
                        pltpu.VMEM((PEER_HEADS * nk // SUBLANES, SUBLANES, tt), F32),
                        pltpu.VMEM((PEER_HEADS * nk // SUBLANES, SUBLANES, tt), F32),
                        pltpu.VMEM((2, PEER_TOPK, PEER_HEADS, tt), F32),
                        pltpu.VMEM((4, PEER_HEADS, tt), F32),
                        pltpu.VMEM((eb, tt), F32),
                        pltpu.VMEM((eb, tt), BF16),
                        pltpu.VMEM((d, tt), F32)],
        compiler_params=_params(("arbitrary", "arbitrary")),
        name="peer",
    )(h1, mod, g2, gf, wq, keys, u_tab, vt_tab)


def kernel(x, c, ada_w, ada_b, norm1_g, w_in, lb_logits, hg_norm_g, w_a, conv_w, conv_b,
           conv_ln_g, conv_ln_b, w_b, w_out, norm2_g, peer_wq, peer_keys, peer_u, peer_v,
           final_g):
    bsz, seq, d = x.shape
    depth = ada_w.shape[0]
    assert depth == 1 and lb_logits.shape[0] == depth + 1
    assert seq % HG_CHUNK == 0 and d % LANES == 0
    t = bsz * seq
    tm = min(512, seq)
    ts = min(256, seq)
    tt = min(512, seq)
    nk = peer_keys.shape[3]
    assert nk == LANES and peer_keys.shape[1] == PEER_HEADS
    eb = SUBLANES * nk
    l = 0

    x2 = x.reshape(t, d)
    ada = _ada(c, ada_w[l].astype(BF16), ada_b[l].reshape(1, -1))
    mod = ada.reshape(bsz, 6, d)

    q, f, v, sg, glu, ga, gb = _in_proj(
        x2, mod, norm1_g[l].reshape(1, d), lb_logits, w_in[l].astype(BF16), seq, tm)
    og = _hgrn2(q, f, v, sg, hg_norm_g[l].reshape(1, d), bsz, seq)
    cv = _conv(glu, conv_w[l], conv_b[l].reshape(1, d), conv_ln_g[l].reshape(1, d),
               conv_ln_b[l].reshape(1, d), bsz, seq, ts, 32)
    h1 = _merge(og, cv, ga, gb, x2, mod, w_a[l].astype(BF16), w_b[l].astype(BF16),
                w_out[l].astype(BF16), seq, tm)
    keys = peer_keys[l].reshape(2 * PEER_HEADS, nk, -1).astype(BF16)
    out = _peer(h1, mod, norm2_g[l].reshape(1, d), final_g.reshape(1, d),
                peer_wq[l].astype(BF16), keys, peer_u[l].astype(BF16),
                peer_v[l].T.astype(BF16), seq, tt, eb, 16)
    return out.reshape(bsz, seq, d)
```

```python
import functools

import jax
import jax.numpy as jnp
from jax import lax
from jax.experimental import pallas as pl
from jax.experimental.pallas import tpu as pltpu

F32 = jnp.float32
BF16 = jnp.bfloat16
EPS = 1e-6
LANES = 128
SUBLANES = 8
HG_HEADS = 8
HG_CHUNK = 128
CONV_K = 31
CONV_HALO = 32
PEER_HEADS = 8
PEER_TOPK = 16
VMEM_LIMIT = 56 * 1024 * 1024

NT_DIMS = (((1,), (1,)), ((), ()))
TN_DIMS = (((0,), (0,)), ((), ()))


def _dot(a, b):
    return jnp.dot(a, b, preferred_element_type=F32)


def _dot_nt(a, b):
    return lax.dot_general(a, b, NT_DIMS, preferred_element_type=F32)


def _rms(x, g):
    ms = jnp.mean(x * x, axis=-1, keepdims=True)
    return x * lax.rsqrt(ms + EPS) * g


def _sigmoid(x):
    return 1.0 / (1.0 + jnp.exp(-x))


def _gelu(x):
    return 0.5 * x * (1.0 + lax.erf(x * (0.5 ** 0.5)))


def _params(sem):
    return pltpu.CompilerParams(dimension_semantics=sem, vmem_limit_bytes=VMEM_LIMIT)


def _ada_kernel(c_ref, w_ref, b_ref, o_ref):
    c = c_ref[...]
    s = (c * _sigmoid(c)).astype(BF16)
    o_ref[...] = _dot(s, w_ref[...]) + b_ref[...]


def _ada(c, ada_w, ada_b):
    bsz, d = c.shape
    n = ada_w.shape[1]
    return pl.pallas_call(
        _ada_kernel,
        grid=(n // d,),
        in_specs=[pl.BlockSpec((bsz, d), lambda j: (0, 0)),
                  pl.BlockSpec((d, d), lambda j: (0, j)),
                  pl.BlockSpec((1, d), lambda j: (0, j))],
        out_specs=pl.BlockSpec((bsz, d), lambda j: (0, j)),
        out_shape=jax.ShapeDtypeStruct((bsz, n), F32),
        compiler_params=_params(("arbitrary",)),
        name="ada",
    )(c, ada_w, ada_b)


def _in_proj_kernel(x_ref, mod_ref, g_ref, lbl_ref, w_ref,
                    q_o, f_o, v_o, sg_o, glu_o, ga_o, gb_o, u_s, zca_s):
    d = x_ref.shape[1]
    m = mod_ref[0]
    u = _rms(x_ref[...], g_ref[...]) * (1.0 + m[1:2]) + m[0:1]
    u_s[...] = u.astype(BF16)

    def z(n):
        return _dot(u_s[...], w_ref[:, n * d:(n + 1) * d])

    q_o[...] = z(0).astype(BF16)
    lg = lbl_ref[...]
    e = jnp.exp(lg - jnp.max(lg, axis=0, keepdims=True))
    lb = e[0:1] / jnp.sum(e, axis=0, keepdims=True)
    f_o[...] = lb + (1.0 - lb) * _sigmoid(z(1))
    v_o[...] = z(2).astype(BF16)
    zg = z(3)
    sg_o[...] = (zg * _sigmoid(zg)).astype(BF16)
    zca_s[...] = z(4)
    glu_o[...] = zca_s[...] * _sigmoid(z(5))
    ga_o[...] = _sigmoid(z(6)).astype(BF16)
    gb_o[...] = _sigmoid(z(7)).astype(BF16)


def _in_proj(x2, mod, g, lb_logits, w_in, seq, tm):
    t, d = x2.shape
    tpb = seq // tm
    row = lambda i: (i, 0)
    out_dt = (BF16, F32, BF16, BF16, F32, BF16, BF16)
    return pl.pallas_call(
        _in_proj_kernel,
        grid=(t // tm,),
        in_specs=[pl.BlockSpec((tm, d), row),
                  pl.BlockSpec((1,) + mod.shape[1:], lambda i: (i // tpb, 0, 0)),
                  pl.BlockSpec((1, d), lambda i: (0, 0)),
                  pl.BlockSpec(lb_logits.shape, lambda i: (0, 0)),
                  pl.BlockSpec(w_in.shape, lambda i: (0, 0), pipeline_mode=pl.Buffered(1))],
        out_specs=[pl.BlockSpec((tm, d), row) for _ in out_dt],
        out_shape=[jax.ShapeDtypeStruct((t, d), dt) for dt in out_dt],
        scratch_shapes=[pltpu.VMEM((tm, d), BF16), pltpu.VMEM((tm, d), F32)],
        compiler_params=_params(("arbitrary",)),
        name="in_proj",
    )(x2, mod, g, lb_logits, w_in)


def _split3(x):
    hi = x.astype(BF16)
    r = x - hi.astype(F32)
    mid = r.astype(BF16)
    lo = (r - mid.astype(F32)).astype(BF16)
    return hi, mid, lo


def _hgrn2_kernel(q_ref, f_ref, v_ref, sg_ref, g_ref, o_ref, *, seq, hd, nheads, unroll):
    c = HG_CHUNK
    rows = lax.broadcasted_iota(jnp.int32, (c, c), 0)
    cols = lax.broadcasted_iota(jnp.int32, (c, c), 1)
    causal = rows >= cols
    ones_tri = jnp.where(causal, 1.0, 0.0).astype(BF16)

    def head_chunk(sl, hs, st, f, b):
        b_mid = b[c // 2 - 1:c // 2, :]
        b_last = b[c - 1:c, :]
        q = q_ref[sl, hs].astype(F32)
        k = 1.0 - f
        v = v_ref[sl, hs]
        qd = (q * jnp.exp(b - b_mid)).astype(BF16)
        kd = (k * jnp.exp(b_mid - b)).astype(BF16)
        scores = jnp.where(causal, _dot_nt(qd, kd), 0.0).astype(BF16)
        o = _dot(scores, v)
        qe = (q * jnp.exp(b)).astype(BF16)
        o = o + _dot_nt(qe, st.astype(BF16))
        kl = (k * jnp.exp(b_last - b)).astype(BF16)
        st = st * jnp.exp(b_last) + lax.dot_general(v, kl, TN_DIMS, preferred_element_type=F32)
        on = _rms(o, g_ref[:, hs])
        o_ref[sl, hs] = (on * sg_ref[sl, hs].astype(F32)).astype(BF16)
        return st

    def chunk(n, states):
        sl = pl.ds(pl.multiple_of(n * c, c), c)
        f_all = f_ref[sl, :]
        hi, mid, lo = _split3(jnp.log(f_all))
        b_all = _dot(ones_tri, hi) + _dot(ones_tri, mid) + _dot(ones_tri, lo)
        out = []
        for h, st in enumerate(states):
            hs = slice(h * hd, (h + 1) * hd)
            out.append(head_chunk(sl, hs, st, f_all[:, hs], b_all[:, hs]))
        return tuple(out)

    lax.fori_loop(0, seq // c, chunk, tuple(jnp.zeros((hd, hd), F32) for _ in range(nheads)),
                  unroll=unroll)


def _hgrn2(q, f, v, sg, g, bsz, seq, nheads, unroll):
    t, d = q.shape
    hd = d // HG_HEADS
    w = nheads * hd
    blk = pl.BlockSpec((seq, w), lambda b, h: (b, h))
    return pl.pallas_call(
        functools.partial(_hgrn2_kernel, seq=seq, hd=hd, nheads=nheads, unroll=unroll),
        grid=(bsz, HG_HEADS // nheads),
        in_specs=[blk, blk, blk, blk, pl.BlockSpec((1, w), lambda b, h: (0, h))],
        out_specs=blk,
        out_shape=jax.ShapeDtypeStruct((t, d), BF16),
        compiler_params=_params(("arbitrary", "arbitrary")),
        name="hgrn2",
    )(q, f, v, sg, g)


def _conv_terms():
    terms = []
    for a in range((CONV_K - 1) // SUBLANES + 2):
        for r in range(SUBLANES):
            hi, lo = SUBLANES * a + r, SUBLANES * a + r - SUBLANES
            if 0 <= hi < CONV_K or (r > 0 and 0 <= lo < CONV_K):
                terms.append((a, r))
    return terms


def _conv_kernel(x_ref, w_ref, b_ref, lg_ref, lb_ref, o_ref, rbuf, wbuf, ybuf, *, ts, mb, cb):
    first = jnp.logical_and(pl.program_id(0) == 0, pl.program_id(1) == 0)
    s = pl.program_id(1)
    halo = CONV_HALO
    d = x_ref.shape[1]
    terms = _conv_terms()

    @pl.when(first)
    def _():
        row = lax.broadcasted_iota(jnp.int32, (SUBLANES, d), 0)
        zero = jnp.zeros((1, d), F32)

        def tap(delay):
            return w_ref[CONV_K - 1 - delay:CONV_K - delay, :] if 0 <= delay < CONV_K else zero

        for ti, (a, r) in enumerate(terms):
            hi = jnp.broadcast_to(tap(SUBLANES * a + r), (SUBLANES, d))
            lo = jnp.broadcast_to(tap(SUBLANES * a + r - SUBLANES), (SUBLANES, d))
            wbuf[ti] = jnp.where(row >= r, hi, lo)

    @pl.when(s == 0)
    def _():
        rbuf[:, 0:halo, :] = jnp.zeros((SUBLANES, halo, d), F32)

    @pl.when(s > 0)
    def _():
        rbuf[:, 0:halo, :] = rbuf[:, ts:ts + halo, :]

    for m in range(ts // SUBLANES):
        xb = x_ref[m * SUBLANES:(m + 1) * SUBLANES, :]
        rbuf[0, halo + m * SUBLANES:halo + (m + 1) * SUBLANES, :] = xb
        for r in range(1, SUBLANES):
            rbuf[r, halo + m * SUBLANES:halo + (m + 1) * SUBLANES, :] = pltpu.roll(xb, r, 0)

    nblk = ts // SUBLANES
    for c0 in range(0, d, cb * LANES):
        cols = [slice(c0 + k * LANES, c0 + (k + 1) * LANES) for k in range(cb)]
        for m0 in range(0, nblk, mb):
            acc = [[jnp.zeros((SUBLANES, LANES), F32) for _ in range(cb)] for _ in range(mb)]
            for ti, (a, r) in enumerate(terms):
                wv = [wbuf[ti, :, cs] for cs in cols]
                for m in range(mb):
                    src = halo + (m0 + m - a) * SUBLANES
                    for k, cs in enumerate(cols):
                        acc[m][k] = acc[m][k] + rbuf[r, src:src + SUBLANES, cs] * wv[k]
            for m in range(mb):
                for k, cs in enumerate(cols):
                    ybuf[(m0 + m) * SUBLANES:(m0 + m + 1) * SUBLANES, cs] = acc[m][k]

    y = ybuf[...] + b_ref[...]
    mu = jnp.mean(y, axis=-1, keepdims=True)
    yc = y - mu
    var = jnp.mean(yc * yc, axis=-1, keepdims=True)
    z = yc * lax.rsqrt(var + EPS) * lg_ref[...] + lb_ref[...]
    o_ref[...] = (z * _sigmoid(z)).astype(BF16)


def _conv(glu, w, b, lg, lb, bsz, seq, ts, mb, cb):
    t, d = glu.shape
    ns = seq // ts
    vec = pl.BlockSpec((1, d), lambda bi, s: (0, 0))
    return pl.pallas_call(
        functools.partial(_conv_kernel, ts=ts, mb=mb, cb=cb),
        grid=(bsz, ns),
        in_specs=[pl.BlockSpec((ts, d), lambda bi, s: (bi * ns + s, 0)),
                  pl.BlockSpec(w.shape, lambda bi, s: (0, 0)),
                  vec, vec, vec],
        out_specs=pl.BlockSpec((ts, d), lambda bi, s: (bi * ns + s, 0)),
        out_shape=jax.ShapeDtypeStruct((t, d), BF16),
        scratch_shapes=[pltpu.VMEM((SUBLANES, ts + CONV_HALO, d), F32),
                        pltpu.VMEM((len(_conv_terms()), SUBLANES, d), F32),
                        pltpu.VMEM((ts, d), F32)],
        compiler_params=_params(("arbitrary", "arbitrary")),
        name="conv",
    )(glu, w, b, lg, lb)


def _merge_kernel(og_ref, cv_ref, ga_ref, gb_ref, x_ref, mod_ref, wa_ref, wb_ref, wo_ref, h_ref):
    ya = _dot(og_ref[...], wa_ref[...])
    yb = _dot(cv_ref[...], wb_ref[...])
    merged = ga_ref[...].astype(F32) * ya + gb_ref[...].astype(F32) * yb
    m = mod_ref[0]
    h_ref[...] = x_ref[...] + m[2:3] * _dot(merged.astype(BF16), wo_ref[...])


def _merge(og, cv, ga, gb, x2, mod, w_a, w_b, w_out, seq, tm):
    t, d = x2.shape
    tpb = seq // tm
    row = pl.BlockSpec((tm, d), lambda i: (i, 0))
    wsp = pl.BlockSpec((d, d), lambda i: (0, 0))
    return pl.pallas_call(
        _merge_kernel,
        grid=(t // tm,),
        in_specs=[row, row, row, row, row,
                  pl.BlockSpec((1,) + mod.shape[1:], lambda i: (i // tpb, 0, 0)),
                  wsp, wsp, wsp],
        out_specs=row,
        out_shape=jax.ShapeDtypeStruct((t, d), F32),
        compiler_params=_params(("arbitrary",)),
        name="merge",
    )(og, cv, ga, gb, x2, mod, w_a, w_b, w_out)


def _top_values(x, k):
    out = []
    for _ in range(k):
        m = jnp.max(x, axis=0, keepdims=True)
        out.append(m)
        x = jnp.where(x == m, -jnp.inf, x)
    return out


def _peer_prep(h_ref, mod_ref, g_ref, wq_ref, keys_ref, u2_s, s_s, e2_s, th_s, e1_s, top_s, pk_s, tt):
    nh = PEER_HEADS
    kk = PEER_TOPK
    nk = keys_ref.shape[1]
    dq = keys_ref.shape[2]
    m = mod_ref[0]
    u2 = _rms(h_ref[...], g_ref[...]) * (1.0 + m[4:5]) + m[3:4]
    u2_s[...] = u2.astype(BF16)
    q = _dot(u2_s[...], wq_ref[...]).astype(BF16)
    for a in range(2 * nh):
        s_s[a] = _dot_nt(keys_ref[a], q[:, a * dq:(a + 1) * dq])

    def per_group(gi, carry):
        ln = pl.ds(pl.multiple_of(gi * LANES, LANES), LANES)
        for a in range(2 * nh):
            vals = _top_values(s_s[a, :, ln], kk + 1)
            for r in range(kk + 1):
                top_s[a % 2, r, a // 2:a // 2 + 1, ln] = vals[r]
        ta = [top_s[0, r, :, ln] for r in range(kk + 1)]
        tb = [top_s[1, r, :, ln] for r in range(kk + 1)]
        cand = [ta[i] + tb[j] for i in range(kk + 1) for j in range((kk + 1) // (i + 1))]
        best = []
        for _ in range(kk + 1):
            mx = cand[0]
            for cv in cand[1:]:
                mx = jnp.maximum(mx, cv)
            best.append(mx)
            cand = [jnp.where(cv == mx, -jnp.inf, cv) for cv in cand]
        c0 = best[0]
        z = jnp.ones_like(c0)
        for cv in best[1:kk]:
            z = z + jnp.exp(cv - c0)
        pk_s[0, :, ln] = 0.5 * (best[kk - 1] + best[kk])
        pk_s[1, :, ln] = ta[0]
        pk_s[2, :, ln] = tb[0]
        pk_s[3, :, ln] = 1.0 / z
        return carry

    lax.fori_loop(0, tt // LANES, per_group, 0)

    ng = nk // SUBLANES
    for hh in range(nh):
        s2 = s_s[2 * hh + 1]
        e2_s[hh] = jnp.exp(s2 - pk_s[2, hh:hh + 1, :])
        for gi in range(ng):
            s1 = s_s[2 * hh, gi * SUBLANES:(gi + 1) * SUBLANES, :]
            th_s[hh * ng + gi] = pk_s[0, hh:hh + 1, :] - s1
            e1_s[hh * ng + gi] = jnp.exp(s1 - pk_s[1, hh:hh + 1, :]) * pk_s[3, hh:hh + 1, :]


def _peer_kernel(h_ref, mod_ref, g2_ref, gf_ref, wq_ref, keys_ref, u_ref, vt_ref, o_ref,
                 u2_s, s_s, e2_s, th_s, e1_s, top_s, pk_s, a_s, hm_s, acc_s, *, tt, eb, rb):
    j = pl.program_id(1)
    nh = PEER_HEADS
    nk = keys_ref.shape[1]
    nrow = eb // nk
    assert nrow == SUBLANES
    ng = nk // SUBLANES

    @pl.when(j == 0)
    def _():
        _peer_prep(h_ref, mod_ref, g2_ref, wq_ref, keys_ref, u2_s, s_s, e2_s, th_s, e1_s, top_s, pk_s, tt)
        acc_s[...] = jnp.zeros_like(acc_s)

    a_s[...] = _dot_nt(u_ref[...], u2_s[...])

    nsub = rb // SUBLANES
    for g0 in range(0, tt, LANES):
        for r0 in range(0, nk, rb):
            gate = [[jnp.zeros((SUBLANES, LANES), F32) for _ in range(nsub)] for _ in range(nrow)]
            for hh in range(nh):
                s2 = [s_s[2 * hh + 1, r0 + k * SUBLANES:r0 + (k + 1) * SUBLANES, g0:g0 + LANES]
                      for k in range(nsub)]
                e2 = [e2_s[hh, r0 + k * SUBLANES:r0 + (k + 1) * SUBLANES, g0:g0 + LANES]
                      for k in range(nsub)]
                for il in range(nrow):
                    th = jnp.broadcast_to(th_s[hh * ng + j, il:il + 1, g0:g0 + LANES], (SUBLANES, LANES))
                    e1 = jnp.broadcast_to(e1_s[hh * ng + j, il:il + 1, g0:g0 + LANES], (SUBLANES, LANES))
                    for k in range(nsub):
                        gate[il][k] = gate[il][k] + jnp.where(s2[k] >= th, e2[k], 0.0) * e1
            for il in range(nrow):
                for k in range(nsub):
                    rr = il * nk + r0 + k * SUBLANES
                    a = a_s[rr:rr + SUBLANES, g0:g0 + LANES]
                    hm_s[rr:rr + SUBLANES, g0:g0 + LANES] = (_gelu(a) * gate[il][k]).astype(BF16)

    acc_s[...] += _dot(vt_ref[...], hm_s[...])

    @pl.when(j == pl.num_programs(1) - 1)
    def _():
        m = mod_ref[0]
        hn = h_ref[...] + m[5:6] * acc_s[...].T
        o_ref[...] = _rms(hn, gf_ref[...])


def _peer(h1, mod, g2, gf, wq, keys, u_tab, vt_tab, seq, tt, eb, rb):
    t, d = h1.shape
    ne = u_tab.shape[0]
    nk = keys.shape[1]
    tpb = seq // tt
    vec = pl.BlockSpec((1, d), lambda i, j: (0, 0))
    return pl.pallas_call(
        functools.partial(_peer_kernel, tt=tt, eb=eb, rb=rb),
        grid=(t // tt, ne // eb),
        in_specs=[pl.BlockSpec((tt, d), lambda i, j: (i, 0)),
                  pl.BlockSpec((1,) + mod.shape[1:], lambda i, j: (i // tpb, 0, 0)),
                  vec, vec,
                  pl.BlockSpec(wq.shape, lambda i, j: (0, 0)),
                  pl.BlockSpec(keys.shape, lambda i, j: (0, 0, 0)),
                  pl.BlockSpec((eb, d), lambda i, j: (j, 0)),
                  pl.BlockSpec((d, eb), lambda i, j: (0, j))],
        out_specs=pl.BlockSpec((tt, d), lambda i, j: (i, 0)),
        out_shape=jax.ShapeDtypeStruct((t, d), F32),
        scratch_shapes=[pltpu.VMEM((tt, d), BF16),
                        pltpu.VMEM((2 * PEER_HEADS, nk, tt), F32),
                        pltpu.VMEM((PEER_HEADS, nk, tt), F32),
                        pltpu.VMEM((PEER_HEADS * nk // SUBLANES, SUBLANES, tt), F32),
                        pltpu.VMEM((PEER_HEADS * nk // SUBLANES, SUBLANES, tt), F32),
                        pltpu.VMEM((2, PEER_TOPK + 1, PEER_HEADS, tt), F32),
                        pltpu.VMEM((4, PEER_HEADS, tt), F32),
                        pltpu.VMEM((eb, tt), F32),
                        pltpu.VMEM((eb, tt), BF16),
                        pltpu.VMEM((d, tt), F32)],
        compiler_params=_params(("arbitrary", "arbitrary")),
        name="peer",
    )(h1, mod, g2, gf, wq, keys, u_tab, vt_tab)


def kernel(x, c, ada_w, ada_b, norm1_g, w_in, lb_logits, hg_norm_g, w_a, conv_w, conv_b,
           conv_ln_g, conv_ln_b, w_b, w_out, norm2_g, peer_wq, peer_keys, peer_u, peer_v,
           final_g):
    bsz, seq, d = x.shape
    depth = ada_w.shape[0]
    assert depth == 1 and lb_logits.shape[0] == depth + 1
    assert seq % HG_CHUNK == 0 and d % LANES == 0
    t = bsz * seq
    tm = min(512, seq)
    ts = min(256, seq)
    tt = min(512, seq)
    nk = peer_keys.shape[3]
    assert nk == LANES and peer_keys.shape[1] == PEER_HEADS
    eb = SUBLANES * nk
    l = 0

    x2 = x.reshape(t, d)
    ada = _ada(c, ada_w[l].astype(BF16), ada_b[l].reshape(1, -1))
    mod = ada.reshape(bsz, 6, d)

    q, f, v, sg, glu, ga, gb = _in_proj(
        x2, mod, norm1_g[l].reshape(1, d), lb_logits, w_in[l].astype(BF16), seq, min(256, seq))
    og = _hgrn2(q, f, v, sg, hg_norm_g[l].reshape(1, d), bsz, seq, HG_HEADS // 2,
                min(4, seq // HG_CHUNK))
    cv = _conv(glu, conv_w[l], conv_b[l].reshape(1, d), conv_ln_g[l].reshape(1, d),
               conv_ln_b[l].reshape(1, d), bsz, seq, ts, 8, 4)
    h1 = _merge(og, cv, ga, gb, x2, mod, w_a[l].astype(BF16), w_b[l].astype(BF16),
                w_out[l].astype(BF16), seq, tm)
    keys = peer_keys[l].reshape(2 * PEER_HEADS, nk, -1).astype(BF16)
    out = _peer(h1, mod, norm2_g[l].reshape(1, d), final_g.reshape(1, d),
                peer_wq[l].astype(BF16), keys, peer_u[l].astype(BF16),
                peer_v[l].T.astype(BF16), seq, tt, eb, 32)
    return out.reshape(bsz, seq, d)
```

```python
import functools

import jax
import jax.numpy as jnp
from jax import lax
from jax.experimental import pallas as pl
from jax.experimental.pallas import tpu as pltpu

F32 = jnp.float32
BF16 = jnp.bfloat16
EPS = 1e-6
LANES = 128
SUBLANES = 8
HG_HEADS = 8
HG_CHUNK = 128
CONV_K = 31
CONV_HALO = 32
PEER_HEADS = 8
PEER_TOPK = 16
LOG2E = 1.4426950408889634
VMEM_LIMIT = 56 * 1024 * 1024

NT_DIMS = (((1,), (1,)), ((), ()))
TN_DIMS = (((0,), (0,)), ((), ()))


def _dot(a, b):
    return jnp.dot(a, b, preferred_element_type=F32)


def _dot_nt(a, b):
    return lax.dot_general(a, b, NT_DIMS, preferred_element_type=F32)


def _rms(x, g):
    ms = jnp.mean(x * x, axis=-1, keepdims=True)
    return x * lax.rsqrt(ms + EPS) * g


def _sigmoid(x):
    return 1.0 / (1.0 + jnp.exp(-x))


def _gelu(x):
    return 0.5 * x * (1.0 + lax.erf(x * (0.5 ** 0.5)))


def _params(sem):
    return pltpu.CompilerParams(dimension_semantics=sem, vmem_limit_bytes=VMEM_LIMIT)


def _ada_kernel(c_ref, w_ref, b_ref, o_ref):
    c = c_ref[...]
    s = (c * _sigmoid(c)).astype(BF16)
    o_ref[...] = _dot(s, w_ref[...]) + b_ref[...]


def _ada(c, ada_w, ada_b):
    bsz, d = c.shape
    n = ada_w.shape[1]
    return pl.pallas_call(
        _ada_kernel,
        grid=(n // d,),
        in_specs=[pl.BlockSpec((bsz, d), lambda j: (0, 0)),
                  pl.BlockSpec((d, d), lambda j: (0, j)),
                  pl.BlockSpec((1, d), lambda j: (0, j))],
        out_specs=pl.BlockSpec((bsz, d), lambda j: (0, j)),
        out_shape=jax.ShapeDtypeStruct((bsz, n), F32),
        compiler_params=_params(("arbitrary",)),
        name="ada",
    )(c, ada_w, ada_b)


def _in_proj_kernel(x_ref, mod_ref, g_ref, lbl_ref, w_ref,
                    q_o, f_o, v_o, sg_o, glu_o, ga_o, gb_o, u_s, zca_s):
    d = x_ref.shape[1]
    m = mod_ref[0]
    u = _rms(x_ref[...], g_ref[...]) * (1.0 + m[1:2]) + m[0:1]
    u_s[...] = u.astype(BF16)

    def z(n):
        return _dot(u_s[...], w_ref[:, n * d:(n + 1) * d])

    q_o[...] = z(0).astype(BF16)
    lg = lbl_ref[...]
    e = jnp.exp(lg - jnp.max(lg, axis=0, keepdims=True))
    lb = e[0:1] / jnp.sum(e, axis=0, keepdims=True)
    f_o[...] = lb + (1.0 - lb) * _sigmoid(z(1))
    v_o[...] = z(2).astype(BF16)
    zg = z(3)
    sg_o[...] = (zg * _sigmoid(zg)).astype(BF16)
    zca_s[...] = z(4)
    glu_o[...] = zca_s[...] * _sigmoid(z(5))
    ga_o[...] = _sigmoid(z(6)).astype(BF16)
    gb_o[...] = _sigmoid(z(7)).astype(BF16)


def _in_proj(x2, mod, g, lb_logits, w_in, seq, tm):
    t, d = x2.shape
    tpb = seq // tm
    row = lambda i: (i, 0)
    out_dt = (BF16, F32, BF16, BF16, F32, BF16, BF16)
    return pl.pallas_call(
        _in_proj_kernel,
        grid=(t // tm,),
        in_specs=[pl.BlockSpec((tm, d), row),
                  pl.BlockSpec((1,) + mod.shape[1:], lambda i: (i // tpb, 0, 0)),
                  pl.BlockSpec((1, d), lambda i: (0, 0)),
                  pl.BlockSpec(lb_logits.shape, lambda i: (0, 0)),
                  pl.BlockSpec(w_in.shape, lambda i: (0, 0), pipeline_mode=pl.Buffered(1))],
        out_specs=[pl.BlockSpec((tm, d), row) for _ in out_dt],
        out_shape=[jax.ShapeDtypeStruct((t, d), dt) for dt in out_dt],
        scratch_shapes=[pltpu.VMEM((tm, d), BF16), pltpu.VMEM((tm, d), F32)],
        compiler_params=_params(("arbitrary",)),
        name="in_proj",
    )(x2, mod, g, lb_logits, w_in)


def _split3(x):
    hi = x.astype(BF16)
    r = x - hi.astype(F32)
    mid = r.astype(BF16)
    lo = (r - mid.astype(F32)).astype(BF16)
    return hi, mid, lo


def _hgrn2_kernel(q_ref, f_ref, v_ref, sg_ref, g_ref, o_ref, *, seq, hd, nheads, unroll):
    c = HG_CHUNK
    rows = lax.broadcasted_iota(jnp.int32, (c, c), 0)
    cols = lax.broadcasted_iota(jnp.int32, (c, c), 1)
    causal = rows >= cols
    ones_tri = jnp.where(causal, 1.0, 0.0).astype(BF16)

    def head_chunk(sl, hs, st, f, b):
        b_mid = b[c // 2 - 1:c // 2, :]
        b_last = b[c - 1:c, :]
        q = q_ref[sl, hs].astype(F32)
        k = 1.0 - f
        v = v_ref[sl, hs]
        qd = (q * jnp.exp(b - b_mid)).astype(BF16)
        kd = (k * jnp.exp(b_mid - b)).astype(BF16)
        scores = jnp.where(causal, _dot_nt(qd, kd), 0.0).astype(BF16)
        o = _dot(scores, v)
        qe = (q * jnp.exp(b)).astype(BF16)
        o = o + _dot_nt(qe, st.astype(BF16))
        kl = (k * jnp.exp(b_last - b)).astype(BF16)
        st = st * jnp.exp(b_last) + lax.dot_general(v, kl, TN_DIMS, preferred_element_type=F32)
        on = _rms(o, g_ref[:, hs])
        o_ref[sl, hs] = (on * sg_ref[sl, hs].astype(F32)).astype(BF16)
        return st

    def chunk(n, states):
        sl = pl.ds(pl.multiple_of(n * c, c), c)
        f_all = f_ref[sl, :]
        hi, mid, lo = _split3(jnp.log(f_all))
        b_all = _dot(ones_tri, hi) + _dot(ones_tri, mid) + _dot(ones_tri, lo)
        out = []
        for h, st in enumerate(states):
            hs = slice(h * hd, (h + 1) * hd)
            out.append(head_chunk(sl, hs, st, f_all[:, hs], b_all[:, hs]))
        return tuple(out)

    lax.fori_loop(0, seq // c, chunk, tuple(jnp.zeros((hd, hd), F32) for _ in range(nheads)),
                  unroll=unroll)


def _hgrn2(q, f, v, sg, g, bsz, seq, nheads, unroll):
    t, d = q.shape
    hd = d // HG_HEADS
    w = nheads * hd
    blk = pl.BlockSpec((seq, w), lambda b, h: (b, h))
    return pl.pallas_call(
        functools.partial(_hgrn2_kernel, seq=seq, hd=hd, nheads=nheads, unroll=unroll),
        grid=(bsz, HG_HEADS // nheads),
        in_specs=[blk, blk, blk, blk, pl.BlockSpec((1, w), lambda b, h: (0, h))],
        out_specs=blk,
        out_shape=jax.ShapeDtypeStruct((t, d), BF16),
        compiler_params=_params(("arbitrary", "arbitrary")),
        name="hgrn2",
    )(q, f, v, sg, g)


def _conv_terms():
    terms = []
    for a in range((CONV_K - 1) // SUBLANES + 2):
        for r in range(SUBLANES):
            hi, lo = SUBLANES * a + r, SUBLANES * a + r - SUBLANES
            if 0 <= hi < CONV_K or (r > 0 and 0 <= lo < CONV_K):
                terms.append((a, r))
    return terms


def _conv_kernel(x_ref, w_ref, b_ref, lg_ref, lb_ref, o_ref, rbuf, wbuf, ybuf, *, ts, mb, cb):
    first = jnp.logical_and(pl.program_id(0) == 0, pl.program_id(1) == 0)
    s = pl.program_id(1)
    halo = CONV_HALO
    d = x_ref.shape[1]
    terms = _conv_terms()

    @pl.when(first)
    def _():
        row = lax.broadcasted_iota(jnp.int32, (SUBLANES, d), 0)
        zero = jnp.zeros((1, d), F32)

        def tap(delay):
            return w_ref[CONV_K - 1 - delay:CONV_K - delay, :] if 0 <= delay < CONV_K else zero

        for ti, (a, r) in enumerate(terms):
            hi = jnp.broadcast_to(tap(SUBLANES * a + r), (SUBLANES, d))
            lo = jnp.broadcast_to(tap(SUBLANES * a + r - SUBLANES), (SUBLANES, d))
            wbuf[ti] = jnp.where(row >= r, hi, lo)

    @pl.when(s == 0)
    def _():
        rbuf[:, 0:halo, :] = jnp.zeros((SUBLANES, halo, d), F32)

    @pl.when(s > 0)
    def _():
        rbuf[:, 0:halo, :] = rbuf[:, ts:ts + halo, :]

    for m in range(ts // SUBLANES):
        xb = x_ref[m * SUBLANES:(m + 1) * SUBLANES, :]
        rbuf[0, halo + m * SUBLANES:halo + (m + 1) * SUBLANES, :] = xb
        for r in range(1, SUBLANES):
            rbuf[r, halo + m * SUBLANES:halo + (m + 1) * SUBLANES, :] = pltpu.roll(xb, r, 0)

    nblk = ts // SUBLANES
    for c0 in range(0, d, cb * LANES):
        cols = [slice(c0 + k * LANES, c0 + (k + 1) * LANES) for k in range(cb)]
        for m0 in range(0, nblk, mb):
            acc = [[jnp.zeros((SUBLANES, LANES), F32) for _ in range(cb)] for _ in range(mb)]
            for ti, (a, r) in enumerate(terms):
                wv = [wbuf[ti, :, cs] for cs in cols]
                for m in range(mb):
                    src = halo + (m0 + m - a) * SUBLANES
                    for k, cs in enumerate(cols):
                        acc[m][k] = acc[m][k] + rbuf[r, src:src + SUBLANES, cs] * wv[k]
            for m in range(mb):
                for k, cs in enumerate(cols):
                    ybuf[(m0 + m) * SUBLANES:(m0 + m + 1) * SUBLANES, cs] = acc[m][k]

    y = ybuf[...] + b_ref[...]
    mu = jnp.mean(y, axis=-1, keepdims=True)
    yc = y - mu
    var = jnp.mean(yc * yc, axis=-1, keepdims=True)
    z = yc * lax.rsqrt(var + EPS) * lg_ref[...] + lb_ref[...]
    o_ref[...] = (z * _sigmoid(z)).astype(BF16)


def _conv(glu, w, b, lg, lb, bsz, seq, ts, mb, cb):
    t, d = glu.shape
    ns = seq // ts
    vec = pl.BlockSpec((1, d), lambda bi, s: (0, 0))
    return pl.pallas_call(
        functools.partial(_conv_kernel, ts=ts, mb=mb, cb=cb),
        grid=(bsz, ns),
        in_specs=[pl.BlockSpec((ts, d), lambda bi, s: (bi * ns + s, 0)),
                  pl.BlockSpec(w.shape, lambda bi, s: (0, 0)),
                  vec, vec, vec],
        out_specs=pl.BlockSpec((ts, d), lambda bi, s: (bi * ns + s, 0)),
        out_shape=jax.ShapeDtypeStruct((t, d), BF16),
        scratch_shapes=[pltpu.VMEM((SUBLANES, ts + CONV_HALO, d), F32),
                        pltpu.VMEM((len(_conv_terms()), SUBLANES, d), F32),
                        pltpu.VMEM((ts, d), F32)],
        compiler_params=_params(("arbitrary", "arbitrary")),
        name="conv",
    )(glu, w, b, lg, lb)


def _merge_kernel(og_ref, cv_ref, ga_ref, gb_ref, x_ref, mod_ref, wa_ref, wb_ref, wo_ref, h_ref):
    ya = _dot(og_ref[...], wa_ref[...])
    yb = _dot(cv_ref[...], wb_ref[...])
    merged = ga_ref[...].astype(F32) * ya + gb_ref[...].astype(F32) * yb
    m = mod_ref[0]
    h_ref[...] = x_ref[...] + m[2:3] * _dot(merged.astype(BF16), wo_ref[...])


def _merge(og, cv, ga, gb, x2, mod, w_a, w_b, w_out, seq, tm):
    t, d = x2.shape
    tpb = seq // tm
    row = pl.BlockSpec((tm, d), lambda i: (i, 0))
    wsp = pl.BlockSpec((d, d), lambda i: (0, 0))
    return pl.pallas_call(
        _merge_kernel,
        grid=(t // tm,),
        in_specs=[row, row, row, row, row,
                  pl.BlockSpec((1,) + mod.shape[1:], lambda i: (i // tpb, 0, 0)),
                  wsp, wsp, wsp],
        out_specs=row,
        out_shape=jax.ShapeDtypeStruct((t, d), F32),
        compiler_params=_params(("arbitrary",)),
        name="merge",
    )(og, cv, ga, gb, x2, mod, w_a, w_b, w_out)


def _top_values(x, k):
    out = []
    for _ in range(k):
        m = jnp.max(x, axis=0, keepdims=True)
        out.append(m)
        x = jnp.where(x == m, -jnp.inf, x)
    return out


def _peer_prep(h_ref, mod_ref, g_ref, wq_ref, keys_ref, u2_s, s_s, s2l_s, phi_s, top_s, pk_s, tt):
    nh = PEER_HEADS
    kk = PEER_TOPK
    nk = keys_ref.shape[1]
    dq = keys_ref.shape[2]
    m = mod_ref[0]
    u2 = _rms(h_ref[...], g_ref[...]) * (1.0 + m[4:5]) + m[3:4]
    u2_s[...] = u2.astype(BF16)
    q = _dot(u2_s[...], wq_ref[...]).astype(BF16)
    for a in range(2 * nh):
        s_s[a] = _dot_nt(keys_ref[a], q[:, a * dq:(a + 1) * dq])

    def per_group(gi, carry):
        ln = pl.ds(pl.multiple_of(gi * LANES, LANES), LANES)
        for a in range(2 * nh):
            vals = _top_values(s_s[a, :, ln], kk + 1)
            for r in range(kk + 1):
                top_s[a % 2, r, a // 2:a // 2 + 1, ln] = vals[r]
        ta = [top_s[0, r, :, ln] for r in range(kk + 1)]
        tb = [top_s[1, r, :, ln] for r in range(kk + 1)]
        cand = [ta[i] + tb[j] for i in range(kk + 1) for j in range((kk + 1) // (i + 1))]
        best = []
        for _ in range(kk + 1):
            mx = cand[0]
            for cv in cand[1:]:
                mx = jnp.maximum(mx, cv)
            best.append(mx)
            cand = [jnp.where(cv == mx, -jnp.inf, cv) for cv in cand]
        c0 = best[0]
        z = jnp.ones_like(c0)
        for cv in best[1:kk]:
            z = z + jnp.exp(cv - c0)
        pk_s[0, :, ln] = 0.5 * (best[kk - 1] + best[kk])
        pk_s[1, :, ln] = c0 + jnp.log(z)
        return carry

    lax.fori_loop(0, tt // LANES, per_group, 0)

    ng = nk // SUBLANES
    for hh in range(nh):
        base = pk_s[1, hh:hh + 1, :]
        pk_s[0, hh:hh + 1, :] = (pk_s[0, hh:hh + 1, :] - base) * LOG2E
        s2l_s[hh] = s_s[2 * hh + 1] * LOG2E
        for gi in range(ng):
            s1 = s_s[2 * hh, gi * SUBLANES:(gi + 1) * SUBLANES, :]
            phi_s[hh * ng + gi] = (s1 - base) * LOG2E


def _peer_kernel(h_ref, mod_ref, g2_ref, gf_ref, wq_ref, keys_ref, u_ref, vt_ref, o_ref,
                 u2_s, s_s, s2l_s, phi_s, top_s, pk_s, a_s, hm_s, acc_s, *, tt, eb, rb):
    j = pl.program_id(1)
    nh = PEER_HEADS
    nk = keys_ref.shape[1]
    nrow = eb // nk
    assert nrow == SUBLANES
    ng = nk // SUBLANES

    @pl.when(j == 0)
    def _():
        _peer_prep(h_ref, mod_ref, g2_ref, wq_ref, keys_ref, u2_s, s_s, s2l_s, phi_s, top_s, pk_s, tt)
        acc_s[...] = jnp.zeros_like(acc_s)

    a_s[...] = _dot_nt(u_ref[...], u2_s[...])

    nsub = rb // SUBLANES
    for g0 in range(0, tt, LANES):
        thr = [jnp.broadcast_to(pk_s[0, hh:hh + 1, g0:g0 + LANES], (SUBLANES, LANES)) for hh in range(nh)]
        for r0 in range(0, nk, rb):
            gate = [[jnp.zeros((SUBLANES, LANES), F32) for _ in range(nsub)] for _ in range(nrow)]
            for hh in range(nh):
                s2 = [s2l_s[hh, r0 + k * SUBLANES:r0 + (k + 1) * SUBLANES, g0:g0 + LANES]
                      for k in range(nsub)]
                for il in range(nrow):
                    phi = jnp.broadcast_to(phi_s[hh * ng + j, il:il + 1, g0:g0 + LANES], (SUBLANES, LANES))
                    for k in range(nsub):
                        lg = s2[k] + phi
                        gate[il][k] = gate[il][k] + jnp.where(lg >= thr[hh], jnp.exp2(lg), 0.0)
            for il in range(nrow):
                for k in range(nsub):
                    rr = il * nk + r0 + k * SUBLANES
                    a = a_s[rr:rr + SUBLANES, g0:g0 + LANES]
                    hm_s[rr:rr + SUBLANES, g0:g0 + LANES] = (_gelu(a) * gate[il][k]).astype(BF16)

    acc_s[...] += _dot(vt_ref[...], hm_s[...])

    @pl.when(j == pl.num_programs(1) - 1)
    def _():
        m = mod_ref[0]
        hn = h_ref[...] + m[5:6] * acc_s[...].T
        o_ref[...] = _rms(hn, gf_ref[...])


def _peer(h1, mod, g2, gf, wq, keys, u_tab, vt_tab, seq, tt, eb, rb):
    t, d = h1.shape
    ne = u_tab.shape[0]
    nk = keys.shape[1]
    tpb = seq // tt
    vec = pl.BlockSpec((1, d), lambda i, j: (0, 0))
    return pl.pallas_call(
        functools.partial(_peer_kernel, tt=tt, eb=eb, rb=rb),
        grid=(t // tt, ne // eb),
        in_specs=[pl.BlockSpec((tt, d), lambda i, j: (i, 0)),
                  pl.BlockSpec((1,) + mod.shape[1:], lambda i, j: (i // tpb, 0, 0)),
                  vec, vec,
                  pl.BlockSpec(wq.shape, lambda i, j: (0, 0)),
                  pl.BlockSpec(keys.shape, lambda i, j: (0, 0, 0)),
                  pl.BlockSpec((eb, d), lambda i, j: (j, 0)),
                  pl.BlockSpec((d, eb), lambda i, j: (0, j))],
        out_specs=pl.BlockSpec((tt, d), lambda i, j: (i, 0)),
        out_shape=jax.ShapeDtypeStruct((t, d), F32),
        scratch_shapes=[pltpu.VMEM((tt, d), BF16),
                        pltpu.VMEM((2 * PEER_HEADS, nk, tt), F32),
                        pltpu.VMEM((PEER_HEADS, nk, tt), F32),
                        pltpu.VMEM((PEER_HEADS * nk // SUBLANES, SUBLANES, tt), F32),
                        pltpu.VMEM((2, PEER_TOPK + 1, PEER_HEADS, tt), F32),
                        pltpu.VMEM((2, PEER_HEADS, tt), F32),
                        pltpu.VMEM((eb, tt), F32),
                        pltpu.VMEM((eb, tt), BF16),
                        pltpu.VMEM((d, tt), F32)],
        compiler_params=_params(("arbitrary", "arbitrary")),
        name="peer",
    )(h1, mod, g2, gf, wq, keys, u_tab, vt_tab)


def kernel(x, c, ada_w, ada_b, norm1_g, w_in, lb_logits, hg_norm_g, w_a, conv_w, conv_b,
           conv_ln_g, conv_ln_b, w_b, w_out, norm2_g, peer_wq, peer_keys, peer_u, peer_v,
           final_g):
    bsz, seq, d = x.shape
    depth = ada_w.shape[0]
    assert depth == 1 and lb_logits.shape[0] == depth + 1
    assert seq % HG_CHUNK == 0 and d % LANES == 0
    t = bsz * seq
    tm = min(512, seq)
    ts = min(256, seq)
    tt = min(512, seq)
    nk = peer_keys.shape[3]
    assert nk == LANES and peer_keys.shape[1] == PEER_HEADS
    eb = SUBLANES * nk
    l = 0

    x2 = x.reshape(t, d)
    ada = _ada(c, ada_w[l].astype(BF16), ada_b[l].reshape(1, -1))
    mod = ada.reshape(bsz, 6, d)

    q, f, v, sg, glu, ga, gb = _in_proj(
        x2, mod, norm1_g[l].reshape(1, d), lb_logits, w_in[l].astype(BF16), seq, min(256, seq))
    og = _hgrn2(q, f, v, sg, hg_norm_g[l].reshape(1, d), bsz, seq, HG_HEADS // 2,
                min(4, seq // HG_CHUNK))
    cv = _conv(glu, conv_w[l], conv_b[l].reshape(1, d), conv_ln_g[l].reshape(1, d),
               conv_ln_b[l].reshape(1, d), bsz, seq, ts, 8, 4)
    h1 = _merge(og, cv, ga, gb, x2, mod, w_a[l].astype(BF16), w_b[l].astype(BF16),
                w_out[l].astype(BF16), seq, tm)
    keys = peer_keys[l].reshape(2 * PEER_HEADS, nk, -1).astype(BF16)
    out = _peer(h1, mod, norm2_g[l].reshape(1, d), final_g.reshape(1, d),
                peer_wq[l].astype(BF16), keys, peer_u[l].astype(BF16),
                peer_v[l].T.astype(BF16), seq, tt, eb, 16)
    return out.reshape(bsz, seq, d)
```

```python
import functools

import jax
import jax.numpy as jnp
from jax import lax
from jax.experimental import pallas as pl
from jax.experimental.pallas import tpu as pltpu

F32 = jnp.float32
BF16 = jnp.bfloat16
EPS = 1e-6
LANES = 128
SUBLANES = 8
HG_HEADS = 8
HG_CHUNK = 128
CONV_K = 31
CONV_HALO = 32
PEER_HEADS = 8
PEER_TOPK = 16
LOG2E = 1.4426950408889634
VMEM_LIMIT = 56 * 1024 * 1024

NT_DIMS = (((1,), (1,)), ((), ()))
TN_DIMS = (((0,), (0,)), ((), ()))


def _dot(a, b):
    return jnp.dot(a, b, preferred_element_type=F32)


def _dot_nt(a, b):
    return lax.dot_general(a, b, NT_DIMS, preferred_element_type=F32)


def _rms(x, g):
    ms = jnp.mean(x * x, axis=-1, keepdims=True)
    return x * lax.rsqrt(ms + EPS) * g


def _sigmoid(x):
    return 1.0 / (1.0 + jnp.exp(-x))


def _gelu(x):
    return 0.5 * x * (1.0 + lax.erf(x * (0.5 ** 0.5)))


def _params(sem):
    return pltpu.CompilerParams(dimension_semantics=sem, vmem_limit_bytes=VMEM_LIMIT)


def _ada_kernel(c_ref, w_ref, b_ref, o_ref):
    c = c_ref[...]
    s = (c * _sigmoid(c)).astype(BF16)
    o_ref[...] = _dot(s, w_ref[...]) + b_ref[...]


def _ada(c, ada_w, ada_b):
    bsz, d = c.shape
    n = ada_w.shape[1]
    return pl.pallas_call(
        _ada_kernel,
        grid=(n // d,),
        in_specs=[pl.BlockSpec((bsz, d), lambda j: (0, 0)),
                  pl.BlockSpec((d, d), lambda j: (0, j)),
                  pl.BlockSpec((1, d), lambda j: (0, j))],
        out_specs=pl.BlockSpec((bsz, d), lambda j: (0, j)),
        out_shape=jax.ShapeDtypeStruct((bsz, n), F32),
        compiler_params=_params(("arbitrary",)),
        name="ada",
    )(c, ada_w, ada_b)


def _in_proj_kernel(x_ref, mod_ref, g_ref, lbl_ref, w_ref,
                    q_o, f_o, v_o, sg_o, glu_o, ga_o, gb_o, u_s, zca_s):
    d = x_ref.shape[1]
    m = mod_ref[0]
    u = _rms(x_ref[...], g_ref[...]) * (1.0 + m[1:2]) + m[0:1]
    u_s[...] = u.astype(BF16)

    def z(n):
        return _dot(u_s[...], w_ref[:, n * d:(n + 1) * d])

    q_o[...] = z(0).astype(BF16)
    lg = lbl_ref[...]
    e = jnp.exp(lg - jnp.max(lg, axis=0, keepdims=True))
    lb = e[0:1] / jnp.sum(e, axis=0, keepdims=True)
    f_o[...] = lb + (1.0 - lb) * _sigmoid(z(1))
    v_o[...] = z(2).astype(BF16)
    zg = z(3)
    sg_o[...] = (zg * _sigmoid(zg)).astype(BF16)
    zca_s[...] = z(4)
    glu_o[...] = zca_s[...] * _sigmoid(z(5))
    ga_o[...] = _sigmoid(z(6)).astype(BF16)
    gb_o[...] = _sigmoid(z(7)).astype(BF16)


def _in_proj(x2, mod, g, lb_logits, w_in, seq, tm):
    t, d = x2.shape
    tpb = seq // tm
    row = lambda i: (i, 0)
    out_dt = (BF16, F32, BF16, BF16, F32, BF16, BF16)
    return pl.pallas_call(
        _in_proj_kernel,
        grid=(t // tm,),
        in_specs=[pl.BlockSpec((tm, d), row),
                  pl.BlockSpec((1,) + mod.shape[1:], lambda i: (i // tpb, 0, 0)),
                  pl.BlockSpec((1, d), lambda i: (0, 0)),
                  pl.BlockSpec(lb_logits.shape, lambda i: (0, 0)),
                  pl.BlockSpec(w_in.shape, lambda i: (0, 0), pipeline_mode=pl.Buffered(1))],
        out_specs=[pl.BlockSpec((tm, d), row) for _ in out_dt],
        out_shape=[jax.ShapeDtypeStruct((t, d), dt) for dt in out_dt],
        scratch_shapes=[pltpu.VMEM((tm, d), BF16), pltpu.VMEM((tm, d), F32)],
        compiler_params=_params(("arbitrary",)),
        name="in_proj",
    )(x2, mod, g, lb_logits, w_in)


def _split3(x):
    hi = x.astype(BF16)
    r = x - hi.astype(F32)
    mid = r.astype(BF16)
    lo = (r - mid.astype(F32)).astype(BF16)
    return hi, mid, lo


def _hgrn2_kernel(q_ref, f_ref, v_ref, sg_ref, g_ref, o_ref, *, seq, hd, nheads, unroll):
    c = HG_CHUNK
    rows = lax.broadcasted_iota(jnp.int32, (c, c), 0)
    cols = lax.broadcasted_iota(jnp.int32, (c, c), 1)
    causal = rows >= cols
    ones_tri = jnp.where(causal, 1.0, 0.0).astype(BF16)

    def head_chunk(sl, hs, st, f, b):
        b_mid = b[c // 2 - 1:c // 2, :]
        b_last = b[c - 1:c, :]
        q = q_ref[sl, hs].astype(F32)
        k = 1.0 - f
        v = v_ref[sl, hs]
        qd = (q * jnp.exp(b - b_mid)).astype(BF16)
        kd = (k * jnp.exp(b_mid - b)).astype(BF16)
        scores = jnp.where(causal, _dot_nt(qd, kd), 0.0).astype(BF16)
        o = _dot(scores, v)
        qe = (q * jnp.exp(b)).astype(BF16)
        o = o + _dot_nt(qe, st.astype(BF16))
        kl = (k * jnp.exp(b_last - b)).astype(BF16)
        st = st * jnp.exp(b_last) + lax.dot_general(v, kl, TN_DIMS, preferred_element_type=F32)
        on = _rms(o, g_ref[:, hs])
        o_ref[sl, hs] = (on * sg_ref[sl, hs].astype(F32)).astype(BF16)
        return st

    def chunk(n, states):
        sl = pl.ds(pl.multiple_of(n * c, c), c)
        f_all = f_ref[sl, :]
        hi, mid, lo = _split3(jnp.log(f_all))
        b_all = _dot(ones_tri, hi) + _dot(ones_tri, mid) + _dot(ones_tri, lo)
        out = []
        for h, st in enumerate(states):
            hs = slice(h * hd, (h + 1) * hd)
            out.append(head_chunk(sl, hs, st, f_all[:, hs], b_all[:, hs]))
        return tuple(out)

    lax.fori_loop(0, seq // c, chunk, tuple(jnp.zeros((hd, hd), F32) for _ in range(nheads)),
                  unroll=unroll)


def _hgrn2(q, f, v, sg, g, bsz, seq, nheads, unroll):
    t, d = q.shape
    hd = d // HG_HEADS
    w = nheads * hd
    blk = pl.BlockSpec((seq, w), lambda b, h: (b, h))
    return pl.pallas_call(
        functools.partial(_hgrn2_kernel, seq=seq, hd=hd, nheads=nheads, unroll=unroll),
        grid=(bsz, HG_HEADS // nheads),
        in_specs=[blk, blk, blk, blk, pl.BlockSpec((1, w), lambda b, h: (0, h))],
        out_specs=blk,
        out_shape=jax.ShapeDtypeStruct((t, d), BF16),
        compiler_params=_params(("arbitrary", "arbitrary")),
        name="hgrn2",
    )(q, f, v, sg, g)


def _conv_terms():
    terms = []
    for a in range((CONV_K - 1) // SUBLANES + 2):
        for r in range(SUBLANES):
            hi, lo = SUBLANES * a + r, SUBLANES * a + r - SUBLANES
            if 0 <= hi < CONV_K or (r > 0 and 0 <= lo < CONV_K):
                terms.append((a, r))
    return terms


def _conv_kernel(x_ref, w_ref, b_ref, lg_ref, lb_ref, o_ref, rbuf, wbuf, ybuf, *, ts, mb, cb):
    first = jnp.logical_and(pl.program_id(0) == 0, pl.program_id(1) == 0)
    s = pl.program_id(1)
    halo = CONV_HALO
    d = x_ref.shape[1]
    terms = _conv_terms()

    @pl.when(first)
    def _():
        row = lax.broadcasted_iota(jnp.int32, (SUBLANES, d), 0)
        zero = jnp.zeros((1, d), F32)

        def tap(delay):
            return w_ref[CONV_K - 1 - delay:CONV_K - delay, :] if 0 <= delay < CONV_K else zero

        for ti, (a, r) in enumerate(terms):
            hi = jnp.broadcast_to(tap(SUBLANES * a + r), (SUBLANES, d))
            lo = jnp.broadcast_to(tap(SUBLANES * a + r - SUBLANES), (SUBLANES, d))
            wbuf[ti] = jnp.where(row >= r, hi, lo)

    @pl.when(s == 0)
    def _():
        rbuf[:, 0:halo, :] = jnp.zeros((SUBLANES, halo, d), F32)

    @pl.when(s > 0)
    def _():
        rbuf[:, 0:halo, :] = rbuf[:, ts:ts + halo, :]

    for m in range(ts // SUBLANES):
        xb = x_ref[m * SUBLANES:(m + 1) * SUBLANES, :]
        rbuf[0, halo + m * SUBLANES:halo + (m + 1) * SUBLANES, :] = xb
        for r in range(1, SUBLANES):
            rbuf[r, halo + m * SUBLANES:halo + (m + 1) * SUBLANES, :] = pltpu.roll(xb, r, 0)

    nblk = ts // SUBLANES
    for c0 in range(0, d, cb * LANES):
        cols = [slice(c0 + k * LANES, c0 + (k + 1) * LANES) for k in range(cb)]
        for m0 in range(0, nblk, mb):
            acc = [[jnp.zeros((SUBLANES, LANES), F32) for _ in range(cb)] for _ in range(mb)]
            for ti, (a, r) in enumerate(terms):
                wv = [wbuf[ti, :, cs] for cs in cols]
                for m in range(mb):
                    src = halo + (m0 + m - a) * SUBLANES
                    for k, cs in enumerate(cols):
                        acc[m][k] = acc[m][k] + rbuf[r, src:src + SUBLANES, cs] * wv[k]
            for m in range(mb):
                for k, cs in enumerate(cols):
                    ybuf[(m0 + m) * SUBLANES:(m0 + m + 1) * SUBLANES, cs] = acc[m][k]

    y = ybuf[...] + b_ref[...]
    mu = jnp.mean(y, axis=-1, keepdims=True)
    yc = y - mu
    var = jnp.mean(yc * yc, axis=-1, keepdims=True)
    z = yc * lax.rsqrt(var + EPS) * lg_ref[...] + lb_ref[...]
    o_ref[...] = (z * _sigmoid(z)).astype(BF16)


def _conv(glu, w, b, lg, lb, bsz, seq, ts, mb, cb):
    t, d = glu.shape
    ns = seq // ts
    vec = pl.BlockSpec((1, d), lambda bi, s: (0, 0))
    return pl.pallas_call(
        functools.partial(_conv_kernel, ts=ts, mb=mb, cb=cb),
        grid=(bsz, ns),
        in_specs=[pl.BlockSpec((ts, d), lambda bi, s: (bi * ns + s, 0)),
                  pl.BlockSpec(w.shape, lambda bi, s: (0, 0)),
                  vec, vec, vec],
        out_specs=pl.BlockSpec((ts, d), lambda bi, s: (bi * ns + s, 0)),
        out_shape=jax.ShapeDtypeStruct((t, d), BF16),
        scratch_shapes=[pltpu.VMEM((SUBLANES, ts + CONV_HALO, d), F32),
                        pltpu.VMEM((len(_conv_terms()), SUBLANES, d), F32),
                        pltpu.VMEM((ts, d), F32)],
        compiler_params=_params(("arbitrary", "arbitrary")),
        name="conv",
    )(glu, w, b, lg, lb)


def _merge_kernel(og_ref, cv_ref, ga_ref, gb_ref, x_ref, mod_ref, wa_ref, wb_ref, wo_ref, h_ref):
    ya = _dot(og_ref[...], wa_ref[...])
    yb = _dot(cv_ref[...], wb_ref[...])
    merged = ga_ref[...].astype(F32) * ya + gb_ref[...].astype(F32) * yb
    m = mod_ref[0]
    h_ref[...] = x_ref[...] + m[2:3] * _dot(merged.astype(BF16), wo_ref[...])


def _merge(og, cv, ga, gb, x2, mod, w_a, w_b, w_out, seq, tm):
    t, d = x2.shape
    tpb = seq // tm
    row = pl.BlockSpec((tm, d), lambda i: (i, 0))
    wsp = pl.BlockSpec((d, d), lambda i: (0, 0))
    return pl.pallas_call(
        _merge_kernel,
        grid=(t // tm,),
        in_specs=[row, row, row, row, row,
                  pl.BlockSpec((1,) + mod.shape[1:], lambda i: (i // tpb, 0, 0)),
                  wsp, wsp, wsp],
        out_specs=row,
        out_shape=jax.ShapeDtypeStruct((t, d), F32),
        compiler_params=_params(("arbitrary",)),
        name="merge",
    )(og, cv, ga, gb, x2, mod, w_a, w_b, w_out)


def _sort_pairs(n):
    pairs = []
    p = 1
    while p < n:
        k = p
        while k >= 1:
            for j in range(k % p, n - k, 2 * k):
                for i in range(min(k, n - j - k)):
                    if (i + j) // (2 * p) == (i + j + k) // (2 * p):
                        pairs.append((i + j, i + j + k))
            k //= 2
        p *= 2
    return pairs


def _bitonic_pairs(n):
    pairs = []
    k = n // 2
    while k >= 1:
        pairs.extend((i, i + k) for i in range(n) if i & k == 0)
        k //= 2
    return pairs


def _exchange(v, pairs):
    for i, j in pairs:
        v[i], v[j] = jnp.maximum(v[i], v[j]), jnp.minimum(v[i], v[j])


def _top_values(tiles, k):
    assert len(tiles) == k and k & (k - 1) == 0
    v = list(tiles)
    _exchange(v, _sort_pairs(k))
    sh = SUBLANES // 2
    while sh >= 1:
        other = [pltpu.roll(t, sh, 0) for t in v]
        v = [jnp.maximum(v[i], other[k - 1 - i]) for i in range(k)]
        _exchange(v, _bitonic_pairs(k))
        sh //= 2
    below = [jnp.where(t < v[k - 1], t, -jnp.inf) for t in tiles]
    nxt = below[0]
    for t in below[1:]:
        nxt = jnp.maximum(nxt, t)
    return [t[0:1, :] for t in v] + [jnp.max(nxt, axis=0, keepdims=True)]


def _peer_prep(h_ref, mod_ref, g_ref, wq_ref, keys_ref, u2_s, s_s, s2l_s, phi_s, top_s, pk_s, tt):
    nh = PEER_HEADS
    kk = PEER_TOPK
    nk = keys_ref.shape[1]
    dq = keys_ref.shape[2]
    m = mod_ref[0]
    u2 = _rms(h_ref[...], g_ref[...]) * (1.0 + m[4:5]) + m[3:4]
    u2_s[...] = u2.astype(BF16)
    q = _dot(u2_s[...], wq_ref[...]).astype(BF16)
    for a in range(2 * nh):
        s_s[a] = _dot_nt(keys_ref[a], q[:, a * dq:(a + 1) * dq])

    def per_group(gi, carry):
        ln = pl.ds(pl.multiple_of(gi * LANES, LANES), LANES)
        for a in range(2 * nh):
            vals = _top_values([s_s[a, i * SUBLANES:(i + 1) * SUBLANES, ln] for i in range(nk // SUBLANES)], kk)
            for r in range(kk + 1):
                top_s[a % 2, r, a // 2:a // 2 + 1, ln] = vals[r]
        ta = [top_s[0, r, :, ln] for r in range(kk + 1)]
        tb = [top_s[1, r, :, ln] for r in range(kk + 1)]
        cand = [ta[i] + tb[j] for i in range(kk + 1) for j in range((kk + 1) // (i + 1))]
        best = []
        for _ in range(kk + 1):
            mx = cand[0]
            for cv in cand[1:]:
                mx = jnp.maximum(mx, cv)
            best.append(mx)
            cand = [jnp.where(cv == mx, -jnp.inf, cv) for cv in cand]
        c0 = best[0]
        z = jnp.ones_like(c0)
        for cv in best[1:kk]:
            z = z + jnp.exp(cv - c0)
        pk_s[0, :, ln] = 0.5 * (best[kk - 1] + best[kk])
        pk_s[1, :, ln] = c0 + jnp.log(z)
        return carry

    lax.fori_loop(0, tt // LANES, per_group, 0)

    ng = nk // SUBLANES
    for hh in range(nh):
        base = pk_s[1, hh:hh + 1, :]
        pk_s[0, hh:hh + 1, :] = (pk_s[0, hh:hh + 1, :] - base) * LOG2E
        s2l_s[hh] = s_s[2 * hh + 1] * LOG2E
        for gi in range(ng):
            s1 = s_s[2 * hh, gi * SUBLANES:(gi + 1) * SUBLANES, :]
            phi_s[hh * ng + gi] = (s1 - base) * LOG2E


def _peer_kernel(h_ref, mod_ref, g2_ref, gf_ref, wq_ref, keys_ref, u_ref, vt_ref, o_ref,
                 u2_s, s_s, s2l_s, phi_s, top_s, pk_s, a_s, hm_s, acc_s, *, tt, eb, rb):
    j = pl.program_id(1)
    nh = PEER_HEADS
    nk = keys_ref.shape[1]
    nrow = eb // nk
    assert nrow == SUBLANES
    ng = nk // SUBLANES

    @pl.when(j == 0)
    def _():
        _peer_prep(h_ref, mod_ref, g2_ref, wq_ref, keys_ref, u2_s, s_s, s2l_s, phi_s, top_s, pk_s, tt)
        acc_s[...] = jnp.zeros_like(acc_s)

    a_s[...] = _dot_nt(u_ref[...], u2_s[...])

    nsub = rb // SUBLANES
    for g0 in range(0, tt, LANES):
        thr = [jnp.broadcast_to(pk_s[0, hh:hh + 1, g0:g0 + LANES], (SUBLANES, LANES)) for hh in range(nh)]
        for r0 in range(0, nk, rb):
            gate = [[jnp.zeros((SUBLANES, LANES), F32) for _ in range(nsub)] for _ in range(nrow)]
            for hh in range(nh):
                s2 = [s2l_s[hh, r0 + k * SUBLANES:r0 + (k + 1) * SUBLANES, g0:g0 + LANES]
                      for k in range(nsub)]
                for il in range(nrow):
                    phi = jnp.broadcast_to(phi_s[hh * ng + j, il:il + 1, g0:g0 + LANES], (SUBLANES, LANES))
                    for k in range(nsub):
                        lg = s2[k] + phi
                        gate[il][k] = gate[il][k] + jnp.where(lg >= thr[hh], jnp.exp2(lg), 0.0)
            for il in range(nrow):
                for k in range(nsub):
                    rr = il * nk + r0 + k * SUBLANES
                    a = a_s[rr:rr + SUBLANES, g0:g0 + LANES]
                    hm_s[rr:rr + SUBLANES, g0:g0 + LANES] = (_gelu(a) * gate[il][k]).astype(BF16)

    acc_s[...] += _dot(vt_ref[...], hm_s[...])

    @pl.when(j == pl.num_programs(1) - 1)
    def _():
        m = mod_ref[0]
        hn = h_ref[...] + m[5:6] * acc_s[...].T
        o_ref[...] = _rms(hn, gf_ref[...])


def _peer(h1, mod, g2, gf, wq, keys, u_tab, vt_tab, seq, tt, eb, rb):
    t, d = h1.shape
    ne = u_tab.shape[0]
    nk = keys.shape[1]
    tpb = seq // tt
    vec = pl.BlockSpec((1, d), lambda i, j: (0, 0))
    return pl.pallas_call(
        functools.partial(_peer_kernel, tt=tt, eb=eb, rb=rb),
        grid=(t // tt, ne // eb),
        in_specs=[pl.BlockSpec((tt, d), lambda i, j: (i, 0)),
                  pl.BlockSpec((1,) + mod.shape[1:], lambda i, j: (i // tpb, 0, 0)),
                  vec, vec,
                  pl.BlockSpec(wq.shape, lambda i, j: (0, 0)),
                  pl.BlockSpec(keys.shape, lambda i, j: (0, 0, 0)),
                  pl.BlockSpec((eb, d), lambda i, j: (j, 0)),
                  pl.BlockSpec((d, eb), lambda i, j: (0, j))],
        out_specs=pl.BlockSpec((tt, d), lambda i, j: (i, 0)),
        out_shape=jax.ShapeDtypeStruct((t, d), F32),
        scratch_shapes=[pltpu.VMEM((tt, d), BF16),
                        pltpu.VMEM((2 * PEER_HEADS, nk, tt), F32),
                        pltpu.VMEM((PEER_HEADS, nk, tt), F32),
                        pltpu.VMEM((PEER_HEADS * nk // SUBLANES, SUBLANES, tt), F32),
                        pltpu.VMEM((2, PEER_TOPK + 1, PEER_HEADS, tt), F32),
                        pltpu.VMEM((2, PEER_HEADS, tt), F32),
                        pltpu.VMEM((eb, tt), F32),
                        pltpu.VMEM((eb, tt), BF16),
                        pltpu.VMEM((d, tt), F32)],
        compiler_params=_params(("arbitrary", "arbitrary")),
        name="peer",
    )(h1, mod, g2, gf, wq, keys, u_tab, vt_tab)


def kernel(x, c, ada_w, ada_b, norm1_g, w_in, lb_logits, hg_norm_g, w_a, conv_w, conv_b,
           conv_ln_g, conv_ln_b, w_b, w_out, norm2_g, peer_wq, peer_keys, peer_u, peer_v,
           final_g):
    bsz, seq, d = x.shape
    depth = ada_w.shape[0]
    assert depth == 1 and lb_logits.shape[0] == depth + 1
    assert seq % HG_CHUNK == 0 and d % LANES == 0
    t = bsz * seq
    tm = min(512, seq)
    ts = min(256, seq)
    tt = min(512, seq)
    nk = peer_keys.shape[3]
    assert nk == LANES and peer_keys.shape[1] == PEER_HEADS
    eb = SUBLANES * nk
    l = 0

    x2 = x.reshape(t, d)
    ada = _ada(c, ada_w[l].astype(BF16), ada_b[l].reshape(1, -1))
    mod = ada.reshape(bsz, 6, d)

    q, f, v, sg, glu, ga, gb = _in_proj(
        x2, mod, norm1_g[l].reshape(1, d), lb_logits, w_in[l].astype(BF16), seq, min(256, seq))
    og = _hgrn2(q, f, v, sg, hg_norm_g[l].reshape(1, d), bsz, seq, HG_HEADS // 2,
                min(4, seq // HG_CHUNK))
    cv = _conv(glu, conv_w[l], conv_b[l].reshape(1, d), conv_ln_g[l].reshape(1, d),
               conv_ln_b[l].reshape(1, d), bsz, seq, ts, 8, 4)
    h1 = _merge(og, cv, ga, gb, x2, mod, w_a[l].astype(BF16), w_b[l].astype(BF16),
                w_out[l].astype(BF16), seq, tm)
    keys = peer_keys[l].reshape(2 * PEER_HEADS, nk, -1).astype(BF16)
    out = _peer(h1, mod, norm2_g[l].reshape(1, d), final_g.reshape(1, d),
                peer_wq[l].astype(BF16), keys, peer_u[l].astype(BF16),
                peer_v[l].T.astype(BF16), seq, tt, eb, 16)
    return out.reshape(bsz, seq, d)
```

```python
import functools

import jax
import jax.numpy as jnp
from jax import lax
from jax.experimental import pallas as pl
from jax.experimental.pallas import tpu as pltpu

F32 = jnp.float32
BF16 = jnp.bfloat16
EPS = 1e-6
LANES = 128
SUBLANES = 8
HG_HEADS = 8
HG_CHUNK = 128
CONV_K = 31
CONV_HALO = 32
PEER_HEADS = 8
PEER_TOPK = 16
LOG2E = 1.4426950408889634
VMEM_LIMIT = 56 * 1024 * 1024

NT_DIMS = (((1,), (1,)), ((), ()))
TN_DIMS = (((0,), (0,)), ((), ()))


def _dot(a, b):
    return jnp.dot(a, b, preferred_element_type=F32)


def _dot_nt(a, b):
    return lax.dot_general(a, b, NT_DIMS, preferred_element_type=F32)


def _rms(x, g):
    ms = jnp.mean(x * x, axis=-1, keepdims=True)
    return x * lax.rsqrt(ms + EPS) * g


def _sigmoid(x):
    return 1.0 / (1.0 + jnp.exp(-x))


def _gelu(x):
    return 0.5 * x * (1.0 + lax.erf(x * (0.5 ** 0.5)))


def _params(sem):
    return pltpu.CompilerParams(dimension_semantics=sem, vmem_limit_bytes=VMEM_LIMIT)


def _ada_kernel(c_ref, w_ref, b_ref, o_ref):
    c = c_ref[...]
    s = (c * _sigmoid(c)).astype(BF16)
    o_ref[...] = _dot(s, w_ref[...]) + b_ref[...]


def _ada(c, ada_w, ada_b):
    bsz, d = c.shape
    n = ada_w.shape[1]
    return pl.pallas_call(
        _ada_kernel,
        grid=(n // d,),
        in_specs=[pl.BlockSpec((bsz, d), lambda j: (0, 0)),
                  pl.BlockSpec((d, d), lambda j: (0, j)),
                  pl.BlockSpec((1, d), lambda j: (0, j))],
        out_specs=pl.BlockSpec((bsz, d), lambda j: (0, j)),
        out_shape=jax.ShapeDtypeStruct((bsz, n), F32),
        compiler_params=_params(("arbitrary",)),
        name="ada",
    )(c, ada_w, ada_b)


def _in_proj_kernel(x_ref, mod_ref, g_ref, lbl_ref, w_ref,
                    q_o, f_o, v_o, sg_o, glu_o, ga_o, gb_o, u_s, zca_s):
    d = x_ref.shape[1]
    m = mod_ref[0]
    u = _rms(x_ref[...], g_ref[...]) * (1.0 + m[1:2]) + m[0:1]
    u_s[...] = u.astype(BF16)

    def z(n):
        return _dot(u_s[...], w_ref[:, n * d:(n + 1) * d])

    q_o[...] = z(0).astype(BF16)
    lg = lbl_ref[...]
    e = jnp.exp(lg - jnp.max(lg, axis=0, keepdims=True))
    lb = e[0:1] / jnp.sum(e, axis=0, keepdims=True)
    f_o[...] = lb + (1.0 - lb) * _sigmoid(z(1))
    v_o[...] = z(2).astype(BF16)
    zg = z(3)
    sg_o[...] = (zg * _sigmoid(zg)).astype(BF16)
    zca_s[...] = z(4)
    glu_o[...] = zca_s[...] * _sigmoid(z(5))
    ga_o[...] = _sigmoid(z(6)).astype(BF16)
    gb_o[...] = _sigmoid(z(7)).astype(BF16)


def _in_proj(x2, mod, g, lb_logits, w_in, seq, tm):
    t, d = x2.shape
    tpb = seq // tm
    row = lambda i: (i, 0)
    out_dt = (BF16, F32, BF16, BF16, F32, BF16, BF16)
    return pl.pallas_call(
        _in_proj_kernel,
        grid=(t // tm,),
        in_specs=[pl.BlockSpec((tm, d), row),
                  pl.BlockSpec((1,) + mod.shape[1:], lambda i: (i // tpb, 0, 0)),
                  pl.BlockSpec((1, d), lambda i: (0, 0)),
                  pl.BlockSpec(lb_logits.shape, lambda i: (0, 0)),
                  pl.BlockSpec(w_in.shape, lambda i: (0, 0), pipeline_mode=pl.Buffered(1))],
        out_specs=[pl.BlockSpec((tm, d), row) for _ in out_dt],
        out_shape=[jax.ShapeDtypeStruct((t, d), dt) for dt in out_dt],
        scratch_shapes=[pltpu.VMEM((tm, d), BF16), pltpu.VMEM((tm, d), F32)],
        compiler_params=_params(("arbitrary",)),
        name="in_proj",
    )(x2, mod, g, lb_logits, w_in)


def _split3(x):
    hi = x.astype(BF16)
    r = x - hi.astype(F32)
    mid = r.astype(BF16)
    lo = (r - mid.astype(F32)).astype(BF16)
    return hi, mid, lo


def _hgrn2_kernel(q_ref, f_ref, v_ref, sg_ref, g_ref, o_ref, *, seq, hd, nheads, unroll):
    c = HG_CHUNK
    rows = lax.broadcasted_iota(jnp.int32, (c, c), 0)
    cols = lax.broadcasted_iota(jnp.int32, (c, c), 1)
    causal = rows >= cols
    ones_tri = jnp.where(causal, 1.0, 0.0).astype(BF16)

    def head_chunk(sl, hs, st, f, b):
        b_mid = b[c // 2 - 1:c // 2, :]
        b_last = b[c - 1:c, :]
        q = q_ref[sl, hs].astype(F32)
        k = 1.0 - f
        v = v_ref[sl, hs]
        qd = (q * jnp.exp(b - b_mid)).astype(BF16)
        kd = (k * jnp.exp(b_mid - b)).astype(BF16)
        scores = jnp.where(causal, _dot_nt(qd, kd), 0.0).astype(BF16)
        o = _dot(scores, v)
        qe = (q * jnp.exp(b)).astype(BF16)
        o = o + _dot_nt(qe, st.astype(BF16))
        kl = (k * jnp.exp(b_last - b)).astype(BF16)
        st = st * jnp.exp(b_last) + lax.dot_general(v, kl, TN_DIMS, preferred_element_type=F32)
        on = _rms(o, g_ref[:, hs])
        o_ref[sl, hs] = (on * sg_ref[sl, hs].astype(F32)).astype(BF16)
        return st

    def chunk(n, states):
        sl = pl.ds(pl.multiple_of(n * c, c), c)
        f_all = f_ref[sl, :]
        hi, mid, lo = _split3(jnp.log(f_all))
        b_all = _dot(ones_tri, hi) + _dot(ones_tri, mid) + _dot(ones_tri, lo)
        out = []
        for h, st in enumerate(states):
            hs = slice(h * hd, (h + 1) * hd)
            out.append(head_chunk(sl, hs, st, f_all[:, hs], b_all[:, hs]))
        return tuple(out)

    lax.fori_loop(0, seq // c, chunk, tuple(jnp.zeros((hd, hd), F32) for _ in range(nheads)),
                  unroll=unroll)


def _hgrn2(q, f, v, sg, g, bsz, seq, nheads, unroll):
    t, d = q.shape
    hd = d // HG_HEADS
    w = nheads * hd
    blk = pl.BlockSpec((seq, w), lambda b, h: (b, h))
    return pl.pallas_call(
        functools.partial(_hgrn2_kernel, seq=seq, hd=hd, nheads=nheads, unroll=unroll),
        grid=(bsz, HG_HEADS // nheads),
        in_specs=[blk, blk, blk, blk, pl.BlockSpec((1, w), lambda b, h: (0, h))],
        out_specs=blk,
        out_shape=jax.ShapeDtypeStruct((t, d), BF16),
        compiler_params=_params(("arbitrary", "arbitrary")),
        name="hgrn2",
    )(q, f, v, sg, g)


def _conv_terms():
    terms = []
    for a in range((CONV_K - 1) // SUBLANES + 2):
        for r in range(SUBLANES):
            hi, lo = SUBLANES * a + r, SUBLANES * a + r - SUBLANES
            if 0 <= hi < CONV_K or (r > 0 and 0 <= lo < CONV_K):
                terms.append((a, r))
    return terms


def _conv_kernel(x_ref, w_ref, b_ref, lg_ref, lb_ref, o_ref, rbuf, wbuf, ybuf, *, ts, mb, cb):
    first = jnp.logical_and(pl.program_id(0) == 0, pl.program_id(1) == 0)
    s = pl.program_id(1)
    halo = CONV_HALO
    d = x_ref.shape[1]
    terms = _conv_terms()

    @pl.when(first)
    def _():
        row = lax.broadcasted_iota(jnp.int32, (SUBLANES, d), 0)
        zero = jnp.zeros((1, d), F32)

        def tap(delay):
            return w_ref[CONV_K - 1 - delay:CONV_K - delay, :] if 0 <= delay < CONV_K else zero

        for ti, (a, r) in enumerate(terms):
            hi = jnp.broadcast_to(tap(SUBLANES * a + r), (SUBLANES, d))
            lo = jnp.broadcast_to(tap(SUBLANES * a + r - SUBLANES), (SUBLANES, d))
            wbuf[ti] = jnp.where(row >= r, hi, lo)

    @pl.when(s == 0)
    def _():
        rbuf[:, 0:halo, :] = jnp.zeros((SUBLANES, halo, d), F32)

    @pl.when(s > 0)
    def _():
        rbuf[:, 0:halo, :] = rbuf[:, ts:ts + halo, :]

    for m in range(ts // SUBLANES):
        xb = x_ref[m * SUBLANES:(m + 1) * SUBLANES, :]
        rbuf[0, halo + m * SUBLANES:halo + (m + 1) * SUBLANES, :] = xb
        for r in range(1, SUBLANES):
            rbuf[r, halo + m * SUBLANES:halo + (m + 1) * SUBLANES, :] = pltpu.roll(xb, r, 0)

    nblk = ts // SUBLANES
    for c0 in range(0, d, cb * LANES):
        cols = [slice(c0 + k * LANES, c0 + (k + 1) * LANES) for k in range(cb)]
        for m0 in range(0, nblk, mb):
            acc = [[jnp.zeros((SUBLANES, LANES), F32) for _ in range(cb)] for _ in range(mb)]
            for ti, (a, r) in enumerate(terms):
                wv = [wbuf[ti, :, cs] for cs in cols]
                for m in range(mb):
                    src = halo + (m0 + m - a) * SUBLANES
                    for k, cs in enumerate(cols):
                        acc[m][k] = acc[m][k] + rbuf[r, src:src + SUBLANES, cs] * wv[k]
            for m in range(mb):
                for k, cs in enumerate(cols):
                    ybuf[(m0 + m) * SUBLANES:(m0 + m + 1) * SUBLANES, cs] = acc[m][k]

    y = ybuf[...] + b_ref[...]
    mu = jnp.mean(y, axis=-1, keepdims=True)
    yc = y - mu
    var = jnp.mean(yc * yc, axis=-1, keepdims=True)
    z = yc * lax.rsqrt(var + EPS) * lg_ref[...] + lb_ref[...]
    o_ref[...] = (z * _sigmoid(z)).astype(BF16)


def _conv(glu, w, b, lg, lb, bsz, seq, ts, mb, cb):
    t, d = glu.shape
    ns = seq // ts
    vec = pl.BlockSpec((1, d), lambda bi, s: (0, 0))
    return pl.pallas_call(
        functools.partial(_conv_kernel, ts=ts, mb=mb, cb=cb),
        grid=(bsz, ns),
        in_specs=[pl.BlockSpec((ts, d), lambda bi, s: (bi * ns + s, 0)),
                  pl.BlockSpec(w.shape, lambda bi, s: (0, 0)),
                  vec, vec, vec],
        out_specs=pl.BlockSpec((ts, d), lambda bi, s: (bi * ns + s, 0)),
        out_shape=jax.ShapeDtypeStruct((t, d), BF16),
        scratch_shapes=[pltpu.VMEM((SUBLANES, ts + CONV_HALO, d), F32),
                        pltpu.VMEM((len(_conv_terms()), SUBLANES, d), F32),
                        pltpu.VMEM((ts, d), F32)],
        compiler_params=_params(("arbitrary", "arbitrary")),
        name="conv",
    )(glu, w, b, lg, lb)


def _merge_kernel(og_ref, cv_ref, ga_ref, gb_ref, x_ref, mod_ref, wa_ref, wb_ref, wo_ref, h_ref):
    ya = _dot(og_ref[...], wa_ref[...])
    yb = _dot(cv_ref[...], wb_ref[...])
    merged = ga_ref[...].astype(F32) * ya + gb_ref[...].astype(F32) * yb
    m = mod_ref[0]
    h_ref[...] = x_ref[...] + m[2:3] * _dot(merged.astype(BF16), wo_ref[...])


def _merge(og, cv, ga, gb, x2, mod, w_a, w_b, w_out, seq, tm):
    t, d = x2.shape
    tpb = seq // tm
    row = pl.BlockSpec((tm, d), lambda i: (i, 0))
    wsp = pl.BlockSpec((d, d), lambda i: (0, 0))
    return pl.pallas_call(
        _merge_kernel,
        grid=(t // tm,),
        in_specs=[row, row, row, row, row,
                  pl.BlockSpec((1,) + mod.shape[1:], lambda i: (i // tpb, 0, 0)),
                  wsp, wsp, wsp],
        out_specs=row,
        out_shape=jax.ShapeDtypeStruct((t, d), F32),
        compiler_params=_params(("arbitrary",)),
        name="merge",
    )(og, cv, ga, gb, x2, mod, w_a, w_b, w_out)


def _sort_pairs(n):
    pairs = []
    p = 1
    while p < n:
        k = p
        while k >= 1:
            for j in range(k % p, n - k, 2 * k):
                for i in range(min(k, n - j - k)):
                    if (i + j) // (2 * p) == (i + j + k) // (2 * p):
                        pairs.append((i + j, i + j + k))
            k //= 2
        p *= 2
    return pairs


def _bitonic_pairs(n):
    pairs = []
    k = n // 2
    while k >= 1:
        pairs.extend((i, i + k) for i in range(n) if i & k == 0)
        k //= 2
    return pairs


def _exchange(v, pairs):
    for i, j in pairs:
        v[i], v[j] = jnp.maximum(v[i], v[j]), jnp.minimum(v[i], v[j])


def _top_values(tiles, k):
    assert len(tiles) == k and k & (k - 1) == 0
    v = list(tiles)
    _exchange(v, _sort_pairs(k))
    sh = SUBLANES // 2
    while sh >= 1:
        other = [pltpu.roll(t, sh, 0) for t in v]
        v = [jnp.maximum(v[i], other[k - 1 - i]) for i in range(k)]
        _exchange(v, _bitonic_pairs(k))
        sh //= 2
    below = [jnp.where(t < v[k - 1], t, -jnp.inf) for t in tiles]
    nxt = below[0]
    for t in below[1:]:
        nxt = jnp.maximum(nxt, t)
    return [t[0:1, :] for t in v] + [jnp.max(nxt, axis=0, keepdims=True)]


def _peer_prep(h_ref, mod_ref, g_ref, wq_ref, keys_ref, u2_s, s_s, s2l_s, phi_s, top_s, pk_s, tt):
    nh = PEER_HEADS
    kk = PEER_TOPK
    nk = keys_ref.shape[1]
    dq = keys_ref.shape[2]
    m = mod_ref[0]
    u2 = _rms(h_ref[...], g_ref[...]) * (1.0 + m[4:5]) + m[3:4]
    u2_s[...] = u2.astype(BF16)
    q = _dot(u2_s[...], wq_ref[...]).astype(BF16)
    for a in range(2 * nh):
        s_s[a] = _dot_nt(keys_ref[a], q[:, a * dq:(a + 1) * dq])

    def per_group(gi, carry):
        ln = pl.ds(pl.multiple_of(gi * LANES, LANES), LANES)
        for a in range(2 * nh):
            vals = _top_values([s_s[a, i * SUBLANES:(i + 1) * SUBLANES, ln] for i in range(nk // SUBLANES)], kk)
            for r in range(kk + 1):
                top_s[a % 2, r, a // 2:a // 2 + 1, ln] = vals[r]
        ta = [top_s[0, r, :, ln] for r in range(kk + 1)]
        tb = [top_s[1, r, :, ln] for r in range(kk + 1)]
        cand = [ta[i] + tb[j] for i in range(kk + 1) for j in range((kk + 1) // (i + 1))]
        best = []
        for _ in range(kk + 1):
            mx = cand[0]
            for cv in cand[1:]:
                mx = jnp.maximum(mx, cv)
            best.append(mx)
            cand = [jnp.where(cv == mx, -jnp.inf, cv) for cv in cand]
        c0 = best[0]
        z = jnp.ones_like(c0)
        for cv in best[1:kk]:
            z = z + jnp.exp(cv - c0)
        pk_s[0, :, ln] = 0.5 * (best[kk - 1] + best[kk])
        pk_s[1, :, ln] = c0 + jnp.log(z)
        return carry

    lax.fori_loop(0, tt // LANES, per_group, 0)

    ng = nk // SUBLANES
    for hh in range(nh):
        base = pk_s[1, hh:hh + 1, :]
        pk_s[0, hh:hh + 1, :] = (pk_s[0, hh:hh + 1, :] - base) * LOG2E
        s2l_s[hh] = s_s[2 * hh + 1] * LOG2E
        for gi in range(ng):
            s1 = s_s[2 * hh, gi * SUBLANES:(gi + 1) * SUBLANES, :]
            phi_s[hh * ng + gi] = (s1 - base) * LOG2E


def _peer_kernel(h_ref, mod_ref, g2_ref, gf_ref, wq_ref, keys_ref, u_ref, v_ref, o_ref,
                 u2_s, s_s, s2l_s, phi_s, top_s, pk_s, a_s, hm_s, acc_s, *, tt, eb, rb):
    j = pl.program_id(1)
    nh = PEER_HEADS
    nk = keys_ref.shape[1]
    nrow = eb // nk
    assert nrow == SUBLANES
    ng = nk // SUBLANES

    @pl.when(j == 0)
    def _():
        _peer_prep(h_ref, mod_ref, g2_ref, wq_ref, keys_ref, u2_s, s_s, s2l_s, phi_s, top_s, pk_s, tt)
        acc_s[...] = jnp.zeros_like(acc_s)

    a_s[...] = _dot_nt(u_ref[...], u2_s[...])

    nsub = rb // SUBLANES
    for g0 in range(0, tt, LANES):
        thr = [jnp.broadcast_to(pk_s[0, hh:hh + 1, g0:g0 + LANES], (SUBLANES, LANES)) for hh in range(nh)]
        for r0 in range(0, nk, rb):
            gate = [[jnp.zeros((SUBLANES, LANES), F32) for _ in range(nsub)] for _ in range(nrow)]
            for hh in range(nh):
                s2 = [s2l_s[hh, r0 + k * SUBLANES:r0 + (k + 1) * SUBLANES, g0:g0 + LANES]
                      for k in range(nsub)]
                for il in range(nrow):
                    phi = jnp.broadcast_to(phi_s[hh * ng + j, il:il + 1, g0:g0 + LANES], (SUBLANES, LANES))
                    for k in range(nsub):
                        lg = s2[k] + phi
                        gate[il][k] = gate[il][k] + jnp.where(lg >= thr[hh], jnp.exp2(lg), 0.0)
            for il in range(nrow):
                for k in range(nsub):
                    rr = il * nk + r0 + k * SUBLANES
                    a = a_s[rr:rr + SUBLANES, g0:g0 + LANES]
                    hm_s[rr:rr + SUBLANES, g0:g0 + LANES] = (_gelu(a) * gate[il][k]).astype(BF16)

    acc_s[...] += lax.dot_general(v_ref[...], hm_s[...], TN_DIMS, preferred_element_type=F32)

    @pl.when(j == pl.num_programs(1) - 1)
    def _():
        m = mod_ref[0]
        hn = h_ref[...] + m[5:6] * acc_s[...].T
        o_ref[...] = _rms(hn, gf_ref[...])


def _peer(h1, mod, g2, gf, wq, keys, u_tab, v_tab, seq, tt, eb, rb):
    t, d = h1.shape
    ne = u_tab.shape[0]
    nk = keys.shape[1]
    tpb = seq // tt
    vec = pl.BlockSpec((1, d), lambda i, j: (0, 0))
    return pl.pallas_call(
        functools.partial(_peer_kernel, tt=tt, eb=eb, rb=rb),
        grid=(t // tt, ne // eb),
        in_specs=[pl.BlockSpec((tt, d), lambda i, j: (i, 0)),
                  pl.BlockSpec((1,) + mod.shape[1:], lambda i, j: (i // tpb, 0, 0)),
                  vec, vec,
                  pl.BlockSpec(wq.shape, lambda i, j: (0, 0)),
                  pl.BlockSpec(keys.shape, lambda i, j: (0, 0, 0)),
                  pl.BlockSpec((eb, d), lambda i, j: (j, 0)),
                  pl.BlockSpec((eb, d), lambda i, j: (j, 0))],
        out_specs=pl.BlockSpec((tt, d), lambda i, j: (i, 0)),
        out_shape=jax.ShapeDtypeStruct((t, d), F32),
        scratch_shapes=[pltpu.VMEM((tt, d), BF16),
                        pltpu.VMEM((2 * PEER_HEADS, nk, tt), F32),
                        pltpu.VMEM((PEER_HEADS, nk, tt), F32),
                        pltpu.VMEM((PEER_HEADS * nk // SUBLANES, SUBLANES, tt), F32),
                        pltpu.VMEM((2, PEER_TOPK + 1, PEER_HEADS, tt), F32),
                        pltpu.VMEM((2, PEER_HEADS, tt), F32),
                        pltpu.VMEM((eb, tt), F32),
                        pltpu.VMEM((eb, tt), BF16),
                        pltpu.VMEM((d, tt), F32)],
        compiler_params=_params(("arbitrary", "arbitrary")),
        name="peer",
    )(h1, mod, g2, gf, wq, keys, u_tab, v_tab)


def kernel(x, c, ada_w, ada_b, norm1_g, w_in, lb_logits, hg_norm_g, w_a, conv_w, conv_b,
           conv_ln_g, conv_ln_b, w_b, w_out, norm2_g, peer_wq, peer_keys, peer_u, peer_v,
           final_g):
    bsz, seq, d = x.shape
    depth = ada_w.shape[0]
    assert depth == 1 and lb_logits.shape[0] == depth + 1
    assert seq % HG_CHUNK == 0 and d % LANES == 0
    t = bsz * seq
    tm = min(512, seq)
    ts = min(256, seq)
    tt = min(512, seq)
    nk = peer_keys.shape[3]
    assert nk == LANES and peer_keys.shape[1] == PEER_HEADS
    eb = SUBLANES * nk
    l = 0

    x2 = x.reshape(t, d)
    ada = _ada(c, ada_w[l].astype(BF16), ada_b[l].reshape(1, -1))
    mod = ada.reshape(bsz, 6, d)

    q, f, v, sg, glu, ga, gb = _in_proj(
        x2, mod, norm1_g[l].reshape(1, d), lb_logits, w_in[l].astype(BF16), seq, min(256, seq))
    og = _hgrn2(q, f, v, sg, hg_norm_g[l].reshape(1, d), bsz, seq, HG_HEADS // 2,
                min(4, seq // HG_CHUNK))
    cv = _conv(glu, conv_w[l], conv_b[l].reshape(1, d), conv_ln_g[l].reshape(1, d),
               conv_ln_b[l].reshape(1, d), bsz, seq, ts, 8, 4)
    h1 = _merge(og, cv, ga, gb, x2, mod, w_a[l].astype(BF16), w_b[l].astype(BF16),
                w_out[l].astype(BF16), seq, tm)
    keys = peer_keys[l].reshape(2 * PEER_HEADS, nk, -1).astype(BF16)
    out = _peer(h1, mod, norm2_g[l].reshape(1, d), final_g.reshape(1, d),
                peer_wq[l].astype(BF16), keys, peer_u[l].astype(BF16),
                peer_v[l].astype(BF16), seq, tt, eb, 16)
    return out.reshape(bsz, seq, d)
```

```python
import functools

import jax
import jax.numpy as jnp
from jax import lax
from jax.experimental import pallas as pl
from jax.experimental.pallas import tpu as pltpu

F32 = jnp.float32
BF16 = jnp.bfloat16
EPS = 1e-6
LANES = 128
SUBLANES = 8
HG_HEADS = 8
HG_CHUNK = 128
CONV_K = 31
CONV_HALO = 32
PEER_HEADS = 8
PEER_TOPK = 16
LOG2E = 1.4426950408889634
VMEM_LIMIT = 56 * 1024 * 1024

NT_DIMS = (((1,), (1,)), ((), ()))
TN_DIMS = (((0,), (0,)), ((), ()))


def _dot(a, b):
    return jnp.dot(a, b, preferred_element_type=F32)


def _dot_nt(a, b):
    return lax.dot_general(a, b, NT_DIMS, preferred_element_type=F32)


def _rms(x, g):
    ms = jnp.mean(x * x, axis=-1, keepdims=True)
    return x * lax.rsqrt(ms + EPS) * g


def _sigmoid(x):
    return 1.0 / (1.0 + jnp.exp(-x))


def _gelu(x):
    return 0.5 * x * (1.0 + lax.erf(x * (0.5 ** 0.5)))


def _params(sem):
    return pltpu.CompilerParams(dimension_semantics=sem, vmem_limit_bytes=VMEM_LIMIT)


def _ada_kernel(c_ref, w_ref, b_ref, o_ref):
    c = c_ref[...]
    s = (c * _sigmoid(c)).astype(BF16)
    o_ref[...] = _dot(s, w_ref[...]) + b_ref[...]


def _ada(c, ada_w, ada_b):
    bsz, d = c.shape
    n = ada_w.shape[1]
    return pl.pallas_call(
        _ada_kernel,
        grid=(n // d,),
        in_specs=[pl.BlockSpec((bsz, d), lambda j: (0, 0)),
                  pl.BlockSpec((d, d), lambda j: (0, j)),
                  pl.BlockSpec((1, d), lambda j: (0, j))],
        out_specs=pl.BlockSpec((bsz, d), lambda j: (0, j)),
        out_shape=jax.ShapeDtypeStruct((bsz, n), F32),
        compiler_params=_params(("arbitrary",)),
        name="ada",
    )(c, ada_w, ada_b)


def _in_proj_kernel(x_ref, mod_ref, g_ref, lbl_ref, w_ref,
                    q_o, f_o, v_o, sg_o, glu_o, ga_o, gb_o, u_s, zca_s):
    d = x_ref.shape[1]
    m = mod_ref[0]
    u = _rms(x_ref[...], g_ref[...]) * (1.0 + m[1:2]) + m[0:1]
    u_s[...] = u.astype(BF16)

    def z(n):
        return _dot(u_s[...], w_ref[:, n * d:(n + 1) * d])

    q_o[...] = z(0).astype(BF16)
    lg = lbl_ref[...]
    e = jnp.exp(lg - jnp.max(lg, axis=0, keepdims=True))
    lb = e[0:1] / jnp.sum(e, axis=0, keepdims=True)
    f_o[...] = lb + (1.0 - lb) * _sigmoid(z(1))
    v_o[...] = z(2).astype(BF16)
    zg = z(3)
    sg_o[...] = (zg * _sigmoid(zg)).astype(BF16)
    zca_s[...] = z(4)
    glu_o[...] = zca_s[...] * _sigmoid(z(5))
    ga_o[...] = _sigmoid(z(6)).astype(BF16)
    gb_o[...] = _sigmoid(z(7)).astype(BF16)


def _in_proj(x2, mod, g, lb_logits, w_in, seq, tm):
    t, d = x2.shape
    tpb = seq // tm
    row = lambda i: (i, 0)
    out_dt = (BF16, F32, BF16, BF16, F32, BF16, BF16)
    return pl.pallas_call(
        _in_proj_kernel,
        grid=(t // tm,),
        in_specs=[pl.BlockSpec((tm, d), row),
                  pl.BlockSpec((1,) + mod.shape[1:], lambda i: (i // tpb, 0, 0)),
                  pl.BlockSpec((1, d), lambda i: (0, 0)),
                  pl.BlockSpec(lb_logits.shape, lambda i: (0, 0)),
                  pl.BlockSpec(w_in.shape, lambda i: (0, 0), pipeline_mode=pl.Buffered(1))],
        out_specs=[pl.BlockSpec((tm, d), row) for _ in out_dt],
        out_shape=[jax.ShapeDtypeStruct((t, d), dt) for dt in out_dt],
        scratch_shapes=[pltpu.VMEM((tm, d), BF16), pltpu.VMEM((tm, d), F32)],
        compiler_params=_params(("arbitrary",)),
        name="in_proj",
    )(x2, mod, g, lb_logits, w_in)


def _split3(x):
    hi = x.astype(BF16)
    r = x - hi.astype(F32)
    mid = r.astype(BF16)
    lo = (r - mid.astype(F32)).astype(BF16)
    return hi, mid, lo


def _hgrn2_kernel(q_ref, f_ref, v_ref, sg_ref, g_ref, o_ref, *, seq, hd, nheads, unroll):
    c = HG_CHUNK
    rows = lax.broadcasted_iota(jnp.int32, (c, c), 0)
    cols = lax.broadcasted_iota(jnp.int32, (c, c), 1)
    causal = rows >= cols
    ones_tri = jnp.where(causal, 1.0, 0.0).astype(BF16)

    def head_chunk(sl, hs, st, f, b):
        b_mid = b[c // 2 - 1:c // 2, :]
        b_last = b[c - 1:c, :]
        q = q_ref[sl, hs].astype(F32)
        k = 1.0 - f
        v = v_ref[sl, hs]
        qd = (q * jnp.exp(b - b_mid)).astype(BF16)
        kd = (k * jnp.exp(b_mid - b)).astype(BF16)
        scores = jnp.where(causal, _dot_nt(qd, kd), 0.0).astype(BF16)
        o = _dot(scores, v)
        qe = (q * jnp.exp(b)).astype(BF16)
        o = o + _dot_nt(qe, st.astype(BF16))
        kl = (k * jnp.exp(b_last - b)).astype(BF16)
        st = st * jnp.exp(b_last) + lax.dot_general(v, kl, TN_DIMS, preferred_element_type=F32)
        on = _rms(o, g_ref[:, hs])
        o_ref[sl, hs] = (on * sg_ref[sl, hs].astype(F32)).astype(BF16)
        return st

    def chunk(n, states):
        sl = pl.ds(pl.multiple_of(n * c, c), c)
        f_all = f_ref[sl, :]
        hi, mid, lo = _split3(jnp.log(f_all))
        b_all = _dot(ones_tri, hi) + _dot(ones_tri, mid) + _dot(ones_tri, lo)
        out = []
        for h, st in enumerate(states):
            hs = slice(h * hd, (h + 1) * hd)
            out.append(head_chunk(sl, hs, st, f_all[:, hs], b_all[:, hs]))
        return tuple(out)

    lax.fori_loop(0, seq // c, chunk, tuple(jnp.zeros((hd, hd), F32) for _ in range(nheads)),
                  unroll=unroll)


def _hgrn2(q, f, v, sg, g, bsz, seq, nheads, unroll):
    t, d = q.shape
    hd = d // HG_HEADS
    w = nheads * hd
    blk = pl.BlockSpec((seq, w), lambda b, h: (b, h))
    return pl.pallas_call(
        functools.partial(_hgrn2_kernel, seq=seq, hd=hd, nheads=nheads, unroll=unroll),
        grid=(bsz, HG_HEADS // nheads),
        in_specs=[blk, blk, blk, blk, pl.BlockSpec((1, w), lambda b, h: (0, h))],
        out_specs=blk,
        out_shape=jax.ShapeDtypeStruct((t, d), BF16),
        compiler_params=_params(("arbitrary", "arbitrary")),
        name="hgrn2",
    )(q, f, v, sg, g)


def _conv_terms():
    terms = []
    for a in range((CONV_K - 1) // SUBLANES + 2):
        for r in range(SUBLANES):
            hi, lo = SUBLANES * a + r, SUBLANES * a + r - SUBLANES
            if 0 <= hi < CONV_K or (r > 0 and 0 <= lo < CONV_K):
                terms.append((a, r))
    return terms


def _conv_kernel(x_ref, w_ref, b_ref, lg_ref, lb_ref, o_ref, rbuf, wbuf, ybuf, *, ts, mb, cb):
    first = jnp.logical_and(pl.program_id(0) == 0, pl.program_id(1) == 0)
    s = pl.program_id(1)
    halo = CONV_HALO
    d = x_ref.shape[1]
    terms = _conv_terms()

    @pl.when(first)
    def _():
        row = lax.broadcasted_iota(jnp.int32, (SUBLANES, d), 0)
        zero = jnp.zeros((1, d), F32)

        def tap(delay):
            return w_ref[CONV_K - 1 - delay:CONV_K - delay, :] if 0 <= delay < CONV_K else zero

        for ti, (a, r) in enumerate(terms):
            hi = jnp.broadcast_to(tap(SUBLANES * a + r), (SUBLANES, d))
            lo = jnp.broadcast_to(tap(SUBLANES * a + r - SUBLANES), (SUBLANES, d))
            wbuf[ti] = jnp.where(row >= r, hi, lo)

    @pl.when(s == 0)
    def _():
        rbuf[:, 0:halo, :] = jnp.zeros((SUBLANES, halo, d), F32)

    @pl.when(s > 0)
    def _():
        rbuf[:, 0:halo, :] = rbuf[:, ts:ts + halo, :]

    for m in range(ts // SUBLANES):
        xb = x_ref[m * SUBLANES:(m + 1) * SUBLANES, :]
        rbuf[0, halo + m * SUBLANES:halo + (m + 1) * SUBLANES, :] = xb
        for r in range(1, SUBLANES):
            rbuf[r, halo + m * SUBLANES:halo + (m + 1) * SUBLANES, :] = pltpu.roll(xb, r, 0)

    nblk = ts // SUBLANES
    for c0 in range(0, d, cb * LANES):
        cols = [slice(c0 + k * LANES, c0 + (k + 1) * LANES) for k in range(cb)]
        for m0 in range(0, nblk, mb):
            acc = [[jnp.zeros((SUBLANES, LANES), F32) for _ in range(cb)] for _ in range(mb)]
            for ti, (a, r) in enumerate(terms):
                wv = [wbuf[ti, :, cs] for cs in cols]
                for m in range(mb):
                    src = halo + (m0 + m - a) * SUBLANES
                    for k, cs in enumerate(cols):
                        acc[m][k] = acc[m][k] + rbuf[r, src:src + SUBLANES, cs] * wv[k]
            for m in range(mb):
                for k, cs in enumerate(cols):
                    ybuf[(m0 + m) * SUBLANES:(m0 + m + 1) * SUBLANES, cs] = acc[m][k]

    y = ybuf[...] + b_ref[...]
    mu = jnp.mean(y, axis=-1, keepdims=True)
    yc = y - mu
    var = jnp.mean(yc * yc, axis=-1, keepdims=True)
    z = yc * lax.rsqrt(var + EPS) * lg_ref[...] + lb_ref[...]
    o_ref[...] = (z * _sigmoid(z)).astype(BF16)


def _conv(glu, w, b, lg, lb, bsz, seq, ts, mb, cb):
    t, d = glu.shape
    ns = seq // ts
    vec = pl.BlockSpec((1, d), lambda bi, s: (0, 0))
    return pl.pallas_call(
        functools.partial(_conv_kernel, ts=ts, mb=mb, cb=cb),
        grid=(bsz, ns),
        in_specs=[pl.BlockSpec((ts, d), lambda bi, s: (bi * ns + s, 0)),
                  pl.BlockSpec(w.shape, lambda bi, s: (0, 0)),
                  vec, vec, vec],
        out_specs=pl.BlockSpec((ts, d), lambda bi, s: (bi * ns + s, 0)),
        out_shape=jax.ShapeDtypeStruct((t, d), BF16),
        scratch_shapes=[pltpu.VMEM((SUBLANES, ts + CONV_HALO, d), F32),
                        pltpu.VMEM((len(_conv_terms()), SUBLANES, d), F32),
                        pltpu.VMEM((ts, d), F32)],
        compiler_params=_params(("arbitrary", "arbitrary")),
        name="conv",
    )(glu, w, b, lg, lb)


def _merge_kernel(og_ref, cv_ref, ga_ref, gb_ref, x_ref, mod_ref, wa_ref, wb_ref, wo_ref, h_ref):
    ya = _dot(og_ref[...], wa_ref[...])
    yb = _dot(cv_ref[...], wb_ref[...])
    merged = ga_ref[...].astype(F32) * ya + gb_ref[...].astype(F32) * yb
    m = mod_ref[0]
    h_ref[...] = x_ref[...] + m[2:3] * _dot(merged.astype(BF16), wo_ref[...])


def _merge(og, cv, ga, gb, x2, mod, w_a, w_b, w_out, seq, tm):
    t, d = x2.shape
    tpb = seq // tm
    row = pl.BlockSpec((tm, d), lambda i: (i, 0))
    wsp = pl.BlockSpec((d, d), lambda i: (0, 0))
    return pl.pallas_call(
        _merge_kernel,
        grid=(t // tm,),
        in_specs=[row, row, row, row, row,
                  pl.BlockSpec((1,) + mod.shape[1:], lambda i: (i // tpb, 0, 0)),
                  wsp, wsp, wsp],
        out_specs=row,
        out_shape=jax.ShapeDtypeStruct((t, d), F32),
        compiler_params=_params(("arbitrary",)),
        name="merge",
    )(og, cv, ga, gb, x2, mod, w_a, w_b, w_out)


def _sort_pairs(n):
    pairs = []
    p = 1
    while p < n:
        k = p
        while k >= 1:
            for j in range(k % p, n - k, 2 * k):
                for i in range(min(k, n - j - k)):
                    if (i + j) // (2 * p) == (i + j + k) // (2 * p):
                        pairs.append((i + j, i + j + k))
            k //= 2
        p *= 2
    return pairs


def _bitonic_pairs(n):
    pairs = []
    k = n // 2
    while k >= 1:
        pairs.extend((i, i + k) for i in range(n) if i & k == 0)
        k //= 2
    return pairs


def _exchange(v, pairs):
    for i, j in pairs:
        v[i], v[j] = jnp.maximum(v[i], v[j]), jnp.minimum(v[i], v[j])


def _top_values(tiles, k):
    assert len(tiles) == k and k & (k - 1) == 0
    v = list(tiles)
    _exchange(v, _sort_pairs(k))
    sh = SUBLANES // 2
    while sh >= 1:
        other = [pltpu.roll(t, sh, 0) for t in v]
        v = [jnp.maximum(v[i], other[k - 1 - i]) for i in range(k)]
        _exchange(v, _bitonic_pairs(k))
        sh //= 2
    below = [jnp.where(t < v[k - 1], t, -jnp.inf) for t in tiles]
    nxt = below[0]
    for t in below[1:]:
        nxt = jnp.maximum(nxt, t)
    return [t[0:1, :] for t in v] + [jnp.max(nxt, axis=0, keepdims=True)]


def _peer_prep(h_ref, mod_ref, g_ref, wq_ref, keys_ref, u2_s, s_s, s2l_s, phi_s, top_s, pk_s, tt):
    nh = PEER_HEADS
    kk = PEER_TOPK
    nk = keys_ref.shape[1]
    dq = keys_ref.shape[2]
    m = mod_ref[0]
    u2 = _rms(h_ref[...], g_ref[...]) * (1.0 + m[4:5]) + m[3:4]
    u2_s[...] = u2.astype(BF16)
    q = _dot(u2_s[...], wq_ref[...]).astype(BF16)
    for a in range(2 * nh):
        s_s[a] = _dot_nt(keys_ref[a], q[:, a * dq:(a + 1) * dq])

    def per_group(gi, carry):
        ln = pl.ds(pl.multiple_of(gi * LANES, LANES), LANES)
        for a in range(2 * nh):
            vals = _top_values([s_s[a, i * SUBLANES:(i + 1) * SUBLANES, ln] for i in range(nk // SUBLANES)], kk)
            for r in range(kk + 1):
                top_s[a % 2, r, a // 2:a // 2 + 1, ln] = vals[r]
        ta = [top_s[0, r, :, ln] for r in range(kk + 1)]
        tb = [top_s[1, r, :, ln] for r in range(kk + 1)]
        cand = [ta[i] + tb[j] for i in range(kk + 1) for j in range((kk + 1) // (i + 1))]
        best = []
        for _ in range(kk + 1):
            mx = cand[0]
            for cv in cand[1:]:
                mx = jnp.maximum(mx, cv)
            best.append(mx)
            cand = [jnp.where(cv == mx, -jnp.inf, cv) for cv in cand]
        c0 = best[0]
        z = jnp.ones_like(c0)
        for cv in best[1:kk]:
            z = z + jnp.exp(cv - c0)
        pk_s[0, :, ln] = 0.5 * (best[kk - 1] + best[kk])
        pk_s[1, :, ln] = c0 + jnp.log(z)
        return carry

    lax.fori_loop(0, tt // LANES, per_group, 0)

    ng = nk // SUBLANES
    for hh in range(nh):
        base = pk_s[1, hh:hh + 1, :]
        pk_s[0, hh:hh + 1, :] = (pk_s[0, hh:hh + 1, :] - base) * LOG2E
        s2l_s[hh] = s_s[2 * hh + 1] * LOG2E
        for gi in range(ng):
            s1 = s_s[2 * hh, gi * SUBLANES:(gi + 1) * SUBLANES, :]
            phi_s[hh * ng + gi] = (s1 - base) * LOG2E


def _peer_kernel(h_ref, mod_ref, g2_ref, gf_ref, wq_ref, keys_ref, u_ref, vt_ref, o_ref,
                 u2_s, s_s, s2l_s, phi_s, top_s, pk_s, a_s, hm_s, acc_s, g_s, *, tt, eb, rb):
    j = pl.program_id(1)
    nh = PEER_HEADS
    nk = keys_ref.shape[1]
    nrow = eb // nk
    assert nrow == SUBLANES
    ng = nk // SUBLANES

    @pl.when(j == 0)
    def _():
        _peer_prep(h_ref, mod_ref, g2_ref, wq_ref, keys_ref, u2_s, s_s, s2l_s, phi_s, top_s, pk_s, tt)
        acc_s[...] = jnp.zeros_like(acc_s)

    nsub = rb // SUBLANES
    for g0 in range(0, tt, LANES):
        thr = [jnp.broadcast_to(pk_s[0, hh:hh + 1, g0:g0 + LANES], (SUBLANES, LANES)) for hh in range(nh)]
        for r0 in range(0, nk, rb):
            gate = [[jnp.zeros((SUBLANES, LANES), F32) for _ in range(nsub)] for _ in range(nrow)]
            for hh in range(nh):
                s2 = [s2l_s[hh, r0 + k * SUBLANES:r0 + (k + 1) * SUBLANES, g0:g0 + LANES]
                      for k in range(nsub)]
                for il in range(nrow):
                    phi = jnp.broadcast_to(phi_s[hh * ng + j, il:il + 1, g0:g0 + LANES], (SUBLANES, LANES))
                    for k in range(nsub):
                        lg = s2[k] + phi
                        gate[il][k] = gate[il][k] + jnp.where(lg >= thr[hh], jnp.exp2(lg), 0.0)
            for il in range(nrow):
                for k in range(nsub):
                    rr = il * nk + r0 + k * SUBLANES
                    g_s[rr:rr + SUBLANES, g0:g0 + LANES] = gate[il][k]

    a_s[...] = _dot_nt(u_ref[...], u2_s[...])
    hm_s[...] = (_gelu(a_s[...]) * g_s[...]).astype(BF16)

    acc_s[...] += _dot(vt_ref[...], hm_s[...])

    @pl.when(j == pl.num_programs(1) - 1)
    def _():
        m = mod_ref[0]
        hn = h_ref[...] + m[5:6] * acc_s[...].T
        o_ref[...] = _rms(hn, gf_ref[...])


def _peer(h1, mod, g2, gf, wq, keys, u_tab, vt_tab, seq, tt, eb, rb):
    t, d = h1.shape
    ne = u_tab.shape[0]
    nk = keys.shape[1]
    tpb = seq // tt
    vec = pl.BlockSpec((1, d), lambda i, j: (0, 0))
    return pl.pallas_call(
        functools.partial(_peer_kernel, tt=tt, eb=eb, rb=rb),
        grid=(t // tt, ne // eb),
        in_specs=[pl.BlockSpec((tt, d), lambda i, j: (i, 0)),
                  pl.BlockSpec((1,) + mod.shape[1:], lambda i, j: (i // tpb, 0, 0)),
                  vec, vec,
                  pl.BlockSpec(wq.shape, lambda i, j: (0, 0)),
                  pl.BlockSpec(keys.shape, lambda i, j: (0, 0, 0)),
                  pl.BlockSpec((eb, d), lambda i, j: (j, 0)),
                  pl.BlockSpec((d, eb), lambda i, j: (0, j))],
        out_specs=pl.BlockSpec((tt, d), lambda i, j: (i, 0)),
        out_shape=jax.ShapeDtypeStruct((t, d), F32),
        scratch_shapes=[pltpu.VMEM((tt, d), BF16),
                        pltpu.VMEM((2 * PEER_HEADS, nk, tt), F32),
                        pltpu.VMEM((PEER_HEADS, nk, tt), F32),
                        pltpu.VMEM((PEER_HEADS * nk // SUBLANES, SUBLANES, tt), F32),
                        pltpu.VMEM((2, PEER_TOPK + 1, PEER_HEADS, tt), F32),
                        pltpu.VMEM((2, PEER_HEADS, tt), F32),
                        pltpu.VMEM((eb, tt), F32),
                        pltpu.VMEM((eb, tt), BF16),
                        pltpu.VMEM((d, tt), F32),
                        pltpu.VMEM((eb, tt), F32)],
        compiler_params=_params(("arbitrary", "arbitrary")),
        name="peer",
    )(h1, mod, g2, gf, wq, keys, u_tab, vt_tab)


def kernel(x, c, ada_w, ada_b, norm1_g, w_in, lb_logits, hg_norm_g, w_a, conv_w, conv_b,
           conv_ln_g, conv_ln_b, w_b, w_out, norm2_g, peer_wq, peer_keys, peer_u, peer_v,
           final_g):
    bsz, seq, d = x.shape
    depth = ada_w.shape[0]
    assert depth == 1 and lb_logits.shape[0] == depth + 1
    assert seq % HG_CHUNK == 0 and d % LANES == 0
    t = bsz * seq
    tm = min(512, seq)
    ts = min(256, seq)
    tt = min(512, seq)
    nk = peer_keys.shape[3]
    assert nk == LANES and peer_keys.shape[1] == PEER_HEADS
    eb = SUBLANES * nk
    l = 0

    x2 = x.reshape(t, d)
    ada = _ada(c, ada_w[l].astype(BF16), ada_b[l].reshape(1, -1))
    mod = ada.reshape(bsz, 6, d)

    q, f, v, sg, glu, ga, gb = _in_proj(
        x2, mod, norm1_g[l].reshape(1, d), lb_logits, w_in[l].astype(BF16), seq, min(256, seq))
    og = _hgrn2(q, f, v, sg, hg_norm_g[l].reshape(1, d), bsz, seq, HG_HEADS // 2,
                min(4, seq // HG_CHUNK))
    cv = _conv(glu, conv_w[l], conv_b[l].reshape(1, d), conv_ln_g[l].reshape(1, d),
               conv_ln_b[l].reshape(1, d), bsz, seq, ts, 8, 4)
    h1 = _merge(og, cv, ga, gb, x2, mod, w_a[l].astype(BF16), w_b[l].astype(BF16),
                w_out[l].astype(BF16), seq, tm)
    keys = peer_keys[l].reshape(2 * PEER_HEADS, nk, -1).astype(BF16)
    out = _peer(h1, mod, norm2_g[l].reshape(1, d), final_g.reshape(1, d),
                peer_wq[l].astype(BF16), keys, peer_u[l].astype(BF16),
                peer_v[l].T.astype(BF16), seq, tt, eb, 16)
    return out.reshape(bsz, seq, d)
```

```python
import functools

import jax
import jax.numpy as jnp
from jax import lax
from jax.experimental import pallas as pl
from jax.experimental.pallas import tpu as pltpu

F32 = jnp.float32
BF16 = jnp.bfloat16
EPS = 1e-6
LANES = 128
SUBLANES = 8
HG_HEADS = 8
HG_CHUNK = 128
CONV_K = 31
CONV_HALO = 32
PEER_HEADS = 8
PEER_TOPK = 16
LOG2E = 1.4426950408889634
VMEM_LIMIT = 56 * 1024 * 1024

NT_DIMS = (((1,), (1,)), ((), ()))
TN_DIMS = (((0,), (0,)), ((), ()))


def _dot(a, b):
    return jnp.dot(a, b, preferred_element_type=F32)


def _dot_nt(a, b):
    return lax.dot_general(a, b, NT_DIMS, preferred_element_type=F32)


def _rms(x, g):
    ms = jnp.mean(x * x, axis=-1, keepdims=True)
    return x * lax.rsqrt(ms + EPS) * g


def _sigmoid(x):
    return 1.0 / (1.0 + jnp.exp(-x))


def _gelu(x):
    return 0.5 * x * (1.0 + lax.erf(x * (0.5 ** 0.5)))


def _params(sem):
    return pltpu.CompilerParams(dimension_semantics=sem, vmem_limit_bytes=VMEM_LIMIT)


def _ada_kernel(c_ref, w_ref, b_ref, o_ref):
    c = c_ref[...]
    s = (c * _sigmoid(c)).astype(BF16)
    o_ref[...] = _dot(s, w_ref[...]) + b_ref[...]


def _ada(c, ada_w, ada_b):
    bsz, d = c.shape
    n = ada_w.shape[1]
    return pl.pallas_call(
        _ada_kernel,
        grid=(n // d,),
        in_specs=[pl.BlockSpec((bsz, d), lambda j: (0, 0)),
                  pl.BlockSpec((d, d), lambda j: (0, j)),
                  pl.BlockSpec((1, d), lambda j: (0, j))],
        out_specs=pl.BlockSpec((bsz, d), lambda j: (0, j)),
        out_shape=jax.ShapeDtypeStruct((bsz, n), F32),
        compiler_params=_params(("arbitrary",)),
        name="ada",
    )(c, ada_w, ada_b)


def _in_proj_kernel(x_ref, mod_ref, g_ref, lbl_ref, w_ref,
                    q_o, f_o, v_o, sg_o, glu_o, ga_o, gb_o, u_s, zca_s):
    d = x_ref.shape[1]
    m = mod_ref[0]
    u = _rms(x_ref[...], g_ref[...]) * (1.0 + m[1:2]) + m[0:1]
    u_s[...] = u.astype(BF16)

    def z(n):
        return _dot(u_s[...], w_ref[:, n * d:(n + 1) * d])

    q_o[...] = z(0).astype(BF16)
    lg = lbl_ref[...]
    e = jnp.exp(lg - jnp.max(lg, axis=0, keepdims=True))
    lb = e[0:1] / jnp.sum(e, axis=0, keepdims=True)
    f_o[...] = lb + (1.0 - lb) * _sigmoid(z(1))
    v_o[...] = z(2).astype(BF16)
    zg = z(3)
    sg_o[...] = (zg * _sigmoid(zg)).astype(BF16)
    zca_s[...] = z(4)
    glu_o[...] = zca_s[...] * _sigmoid(z(5))
    ga_o[...] = _sigmoid(z(6)).astype(BF16)
    gb_o[...] = _sigmoid(z(7)).astype(BF16)


def _in_proj(x2, mod, g, lb_logits, w_in, seq, tm):
    t, d = x2.shape
    tpb = seq // tm
    row = lambda i: (i, 0)
    out_dt = (BF16, F32, BF16, BF16, F32, BF16, BF16)
    return pl.pallas_call(
        _in_proj_kernel,
        grid=(t // tm,),
        in_specs=[pl.BlockSpec((tm, d), row),
                  pl.BlockSpec((1,) + mod.shape[1:], lambda i: (i // tpb, 0, 0)),
                  pl.BlockSpec((1, d), lambda i: (0, 0)),
                  pl.BlockSpec(lb_logits.shape, lambda i: (0, 0)),
                  pl.BlockSpec(w_in.shape, lambda i: (0, 0), pipeline_mode=pl.Buffered(1))],
        out_specs=[pl.BlockSpec((tm, d), row) for _ in out_dt],
        out_shape=[jax.ShapeDtypeStruct((t, d), dt) for dt in out_dt],
        scratch_shapes=[pltpu.VMEM((tm, d), BF16), pltpu.VMEM((tm, d), F32)],
        compiler_params=_params(("arbitrary",)),
        name="in_proj",
    )(x2, mod, g, lb_logits, w_in)


def _split3(x):
    hi = x.astype(BF16)
    r = x - hi.astype(F32)
    mid = r.astype(BF16)
    lo = (r - mid.astype(F32)).astype(BF16)
    return hi, mid, lo


def _hgrn2_kernel(q_ref, f_ref, v_ref, sg_ref, g_ref, o_ref, *, seq, hd, nheads, unroll):
    c = HG_CHUNK
    rows = lax.broadcasted_iota(jnp.int32, (c, c), 0)
    cols = lax.broadcasted_iota(jnp.int32, (c, c), 1)
    causal = rows >= cols
    ones_tri = jnp.where(causal, 1.0, 0.0).astype(BF16)

    def head_chunk(sl, hs, st, f, b):
        b_mid = b[c // 2 - 1:c // 2, :]
        b_last = b[c - 1:c, :]
        q = q_ref[sl, hs].astype(F32)
        k = 1.0 - f
        v = v_ref[sl, hs]
        qd = (q * jnp.exp(b - b_mid)).astype(BF16)
        kd = (k * jnp.exp(b_mid - b)).astype(BF16)
        scores = jnp.where(causal, _dot_nt(qd, kd), 0.0).astype(BF16)
        o = _dot(scores, v)
        qe = (q * jnp.exp(b)).astype(BF16)
        o = o + _dot_nt(qe, st.astype(BF16))
        kl = (k * jnp.exp(b_last - b)).astype(BF16)
        st = st * jnp.exp(b_last) + lax.dot_general(v, kl, TN_DIMS, preferred_element_type=F32)
        on = _rms(o, g_ref[:, hs])
        o_ref[sl, hs] = (on * sg_ref[sl, hs].astype(F32)).astype(BF16)
        return st

    def chunk(n, states):
        sl = pl.ds(pl.multiple_of(n * c, c), c)
        f_all = f_ref[sl, :]
        hi, mid, lo = _split3(jnp.log(f_all))
        b_all = _dot(ones_tri, hi) + _dot(ones_tri, mid) + _dot(ones_tri, lo)
        out = []
        for h, st in enumerate(states):
            hs = slice(h * hd, (h + 1) * hd)
            out.append(head_chunk(sl, hs, st, f_all[:, hs], b_all[:, hs]))
        return tuple(out)

    lax.fori_loop(0, seq // c, chunk, tuple(jnp.zeros((hd, hd), F32) for _ in range(nheads)),
                  unroll=unroll)


def _hgrn2(q, f, v, sg, g, bsz, seq, nheads, unroll):
    t, d = q.shape
    hd = d // HG_HEADS
    w = nheads * hd
    blk = pl.BlockSpec((seq, w), lambda b, h: (b, h))
    return pl.pallas_call(
        functools.partial(_hgrn2_kernel, seq=seq, hd=hd, nheads=nheads, unroll=unroll),
        grid=(bsz, HG_HEADS // nheads),
        in_specs=[blk, blk, blk, blk, pl.BlockSpec((1, w), lambda b, h: (0, h))],
        out_specs=blk,
        out_shape=jax.ShapeDtypeStruct((t, d), BF16),
        compiler_params=_params(("arbitrary", "arbitrary")),
        name="hgrn2",
    )(q, f, v, sg, g)


def _conv_terms():
    terms = []
    for a in range((CONV_K - 1) // SUBLANES + 2):
        for r in range(SUBLANES):
            hi, lo = SUBLANES * a + r, SUBLANES * a + r - SUBLANES
            if 0 <= hi < CONV_K or (r > 0 and 0 <= lo < CONV_K):
                terms.append((a, r))
    return terms


def _conv_kernel(x_ref, w_ref, b_ref, lg_ref, lb_ref, o_ref, rbuf, wbuf, ybuf, *, ts, mb, cb):
    first = jnp.logical_and(pl.program_id(0) == 0, pl.program_id(1) == 0)
    s = pl.program_id(1)
    halo = CONV_HALO
    d = x_ref.shape[1]
    terms = _conv_terms()

    @pl.when(first)
    def _():
        row = lax.broadcasted_iota(jnp.int32, (SUBLANES, d), 0)
        zero = jnp.zeros((1, d), F32)

        def tap(delay):
            return w_ref[CONV_K - 1 - delay:CONV_K - delay, :] if 0 <= delay < CONV_K else zero

        for ti, (a, r) in enumerate(terms):
            hi = jnp.broadcast_to(tap(SUBLANES * a + r), (SUBLANES, d))
            lo = jnp.broadcast_to(tap(SUBLANES * a + r - SUBLANES), (SUBLANES, d))
            wbuf[ti] = jnp.where(row >= r, hi, lo)

    @pl.when(s == 0)
    def _():
        rbuf[:, 0:halo, :] = jnp.zeros((SUBLANES, halo, d), F32)

    @pl.when(s > 0)
    def _():
        rbuf[:, 0:halo, :] = rbuf[:, ts:ts + halo, :]

    for m in range(ts // SUBLANES):
        xb = x_ref[m * SUBLANES:(m + 1) * SUBLANES, :]
        rbuf[0, halo + m * SUBLANES:halo + (m + 1) * SUBLANES, :] = xb
        for r in range(1, SUBLANES):
            rbuf[r, halo + m * SUBLANES:halo + (m + 1) * SUBLANES, :] = pltpu.roll(xb, r, 0)

    nblk = ts // SUBLANES
    for c0 in range(0, d, cb * LANES):
        cols = [slice(c0 + k * LANES, c0 + (k + 1) * LANES) for k in range(cb)]
        for m0 in range(0, nblk, mb):
            acc = [[jnp.zeros((SUBLANES, LANES), F32) for _ in range(cb)] for _ in range(mb)]
            for ti, (a, r) in enumerate(terms):
                wv = [wbuf[ti, :, cs] for cs in cols]
                for m in range(mb):
                    src = halo + (m0 + m - a) * SUBLANES
                    for k, cs in enumerate(cols):
                        acc[m][k] = acc[m][k] + rbuf[r, src:src + SUBLANES, cs] * wv[k]
            for m in range(mb):
                for k, cs in enumerate(cols):
                    ybuf[(m0 + m) * SUBLANES:(m0 + m + 1) * SUBLANES, cs] = acc[m][k]

    y = ybuf[...] + b_ref[...]
    mu = jnp.mean(y, axis=-1, keepdims=True)
    yc = y - mu
    var = jnp.mean(yc * yc, axis=-1, keepdims=True)
    z = yc * lax.rsqrt(var + EPS) * lg_ref[...] + lb_ref[...]
    o_ref[...] = (z * _sigmoid(z)).astype(BF16)


def _conv(glu, w, b, lg, lb, bsz, seq, ts, mb, cb):
    t, d = glu.shape
    ns = seq // ts
    vec = pl.BlockSpec((1, d), lambda bi, s: (0, 0))
    return pl.pallas_call(
        functools.partial(_conv_kernel, ts=ts, mb=mb, cb=cb),
        grid=(bsz, ns),
        in_specs=[pl.BlockSpec((ts, d), lambda bi, s: (bi * ns + s, 0)),
                  pl.BlockSpec(w.shape, lambda bi, s: (0, 0)),
                  vec, vec, vec],
        out_specs=pl.BlockSpec((ts, d), lambda bi, s: (bi * ns + s, 0)),
        out_shape=jax.ShapeDtypeStruct((t, d), BF16),
        scratch_shapes=[pltpu.VMEM((SUBLANES, ts + CONV_HALO, d), F32),
                        pltpu.VMEM((len(_conv_terms()), SUBLANES, d), F32),
                        pltpu.VMEM((ts, d), F32)],
        compiler_params=_params(("arbitrary", "arbitrary")),
        name="conv",
    )(glu, w, b, lg, lb)


def _merge_kernel(og_ref, cv_ref, ga_ref, gb_ref, x_ref, mod_ref, wa_ref, wb_ref, wo_ref, h_ref):
    ya = _dot(og_ref[...], wa_ref[...])
    yb = _dot(cv_ref[...], wb_ref[...])
    merged = ga_ref[...].astype(F32) * ya + gb_ref[...].astype(F32) * yb
    m = mod_ref[0]
    h_ref[...] = x_ref[...] + m[2:3] * _dot(merged.astype(BF16), wo_ref[...])


def _merge(og, cv, ga, gb, x2, mod, w_a, w_b, w_out, seq, tm):
    t, d = x2.shape
    tpb = seq // tm
    row = pl.BlockSpec((tm, d), lambda i: (i, 0))
    wsp = pl.BlockSpec((d, d), lambda i: (0, 0), pipeline_mode=pl.Buffered(1))
    return pl.pallas_call(
        _merge_kernel,
        grid=(t // tm,),
        in_specs=[row, row, row, row, row,
                  pl.BlockSpec((1,) + mod.shape[1:], lambda i: (i // tpb, 0, 0)),
                  wsp, wsp, wsp],
        out_specs=row,
        out_shape=jax.ShapeDtypeStruct((t, d), F32),
        compiler_params=_params(("arbitrary",)),
        name="merge",
    )(og, cv, ga, gb, x2, mod, w_a, w_b, w_out)


def _sort_pairs(n):
    pairs = []
    p = 1
    while p < n:
        k = p
        while k >= 1:
            for j in range(k % p, n - k, 2 * k):
                for i in range(min(k, n - j - k)):
                    if (i + j) // (2 * p) == (i + j + k) // (2 * p):
                        pairs.append((i + j, i + j + k))
            k //= 2
        p *= 2
    return pairs


def _bitonic_pairs(n):
    pairs = []
    k = n // 2
    while k >= 1:
        pairs.extend((i, i + k) for i in range(n) if i & k == 0)
        k //= 2
    return pairs


def _exchange(v, pairs):
    for i, j in pairs:
        v[i], v[j] = jnp.maximum(v[i], v[j]), jnp.minimum(v[i], v[j])


def _top_values(tiles, k):
    assert len(tiles) == k and k & (k - 1) == 0
    v = list(tiles)
    _exchange(v, _sort_pairs(k))
    sh = SUBLANES // 2
    while sh >= 1:
        other = [pltpu.roll(t, sh, 0) for t in v]
        v = [jnp.maximum(v[i], other[k - 1 - i]) for i in range(k)]
        _exchange(v, _bitonic_pairs(k))
        sh //= 2
    below = [jnp.where(t < v[k - 1], t, -jnp.inf) for t in tiles]
    nxt = below[0]
    for t in below[1:]:
        nxt = jnp.maximum(nxt, t)
    return [t[0:1, :] for t in v] + [jnp.max(nxt, axis=0, keepdims=True)]


def _peer_prep(h_ref, mod_ref, g_ref, wq_ref, keys_ref, u2_s, s_s, s2l_s, phi_s, top_s, pk_s, tt):
    nh = PEER_HEADS
    kk = PEER_TOPK
    nk = keys_ref.shape[1]
    dq = keys_ref.shape[2]
    m = mod_ref[0]
    u2 = _rms(h_ref[...], g_ref[...]) * (1.0 + m[4:5]) + m[3:4]
    u2_s[...] = u2.astype(BF16)
    q = _dot(u2_s[...], wq_ref[...]).astype(BF16)
    for a in range(2 * nh):
        s_s[a] = _dot_nt(keys_ref[a], q[:, a * dq:(a + 1) * dq])

    def per_group(gi, carry):
        ln = pl.ds(pl.multiple_of(gi * LANES, LANES), LANES)
        for a in range(2 * nh):
            vals = _top_values([s_s[a, i * SUBLANES:(i + 1) * SUBLANES, ln] for i in range(nk // SUBLANES)], kk)
            for r in range(kk + 1):
                top_s[a % 2, r, a // 2:a // 2 + 1, ln] = vals[r]
        ta = [top_s[0, r, :, ln] for r in range(kk + 1)]
        tb = [top_s[1, r, :, ln] for r in range(kk + 1)]
        cand = [ta[i] + tb[j] for i in range(kk + 1) for j in range((kk + 1) // (i + 1))]
        best = []
        for _ in range(kk + 1):
            mx = cand[0]
            for cv in cand[1:]:
                mx = jnp.maximum(mx, cv)
            best.append(mx)
            cand = [jnp.where(cv == mx, -jnp.inf, cv) for cv in cand]
        c0 = best[0]
        z = jnp.ones_like(c0)
        for cv in best[1:kk]:
            z = z + jnp.exp(cv - c0)
        pk_s[0, :, ln] = 0.5 * (best[kk - 1] + best[kk])
        pk_s[1, :, ln] = c0 + jnp.log(z)
        return carry

    lax.fori_loop(0, tt // LANES, per_group, 0)

    ng = nk // SUBLANES
    for hh in range(nh):
        base = pk_s[1, hh:hh + 1, :]
        pk_s[0, hh:hh + 1, :] = (pk_s[0, hh:hh + 1, :] - base) * LOG2E
        s2l_s[hh] = s_s[2 * hh + 1] * LOG2E
        for gi in range(ng):
            s1 = s_s[2 * hh, gi * SUBLANES:(gi + 1) * SUBLANES, :]
            phi_s[hh * ng + gi] = (s1 - base) * LOG2E


def _peer_kernel(h_ref, mod_ref, g2_ref, gf_ref, wq_ref, keys_ref, u_ref, vt_ref, o_ref,
                 u2_s, s_s, s2l_s, phi_s, top_s, pk_s, a_s, hm_s, acc_s, g_s, *, tt, eb, rb):
    j = pl.program_id(1)
    nh = PEER_HEADS
    nk = keys_ref.shape[1]
    nrow = eb // nk
    assert nrow == SUBLANES
    ng = nk // SUBLANES

    @pl.when(j == 0)
    def _():
        _peer_prep(h_ref, mod_ref, g2_ref, wq_ref, keys_ref, u2_s, s_s, s2l_s, phi_s, top_s, pk_s, tt)
        acc_s[...] = jnp.zeros_like(acc_s)

    nsub = rb // SUBLANES
    for g0 in range(0, tt, LANES):
        thr = [jnp.broadcast_to(pk_s[0, hh:hh + 1, g0:g0 + LANES], (SUBLANES, LANES)) for hh in range(nh)]
        for r0 in range(0, nk, rb):
            gate = [[jnp.zeros((SUBLANES, LANES), F32) for _ in range(nsub)] for _ in range(nrow)]
            for hh in range(nh):
                s2 = [s2l_s[hh, r0 + k * SUBLANES:r0 + (k + 1) * SUBLANES, g0:g0 + LANES]
                      for k in range(nsub)]
                for il in range(nrow):
                    phi = jnp.broadcast_to(phi_s[hh * ng + j, il:il + 1, g0:g0 + LANES], (SUBLANES, LANES))
                    for k in range(nsub):
                        lg = s2[k] + phi
                        gate[il][k] = gate[il][k] + jnp.where(lg >= thr[hh], jnp.exp2(lg), 0.0)
            for il in range(nrow):
                for k in range(nsub):
                    rr = il * nk + r0 + k * SUBLANES
                    g_s[rr:rr + SUBLANES, g0:g0 + LANES] = gate[il][k]

    a_s[...] = _dot_nt(u_ref[...], u2_s[...])
    hm_s[...] = (_gelu(a_s[...]) * g_s[...]).astype(BF16)

    acc_s[...] += _dot(vt_ref[...], hm_s[...])

    @pl.when(j == pl.num_programs(1) - 1)
    def _():
        m = mod_ref[0]
        hn = h_ref[...] + m[5:6] * acc_s[...].T
        o_ref[...] = _rms(hn, gf_ref[...])


def _peer(h1, mod, g2, gf, wq, keys, u_tab, vt_tab, seq, tt, eb, rb):
    t, d = h1.shape
    ne = u_tab.shape[0]
    nk = keys.shape[1]
    tpb = seq // tt
    vec = pl.BlockSpec((1, d), lambda i, j: (0, 0))
    return pl.pallas_call(
        functools.partial(_peer_kernel, tt=tt, eb=eb, rb=rb),
        grid=(t // tt, ne // eb),
        in_specs=[pl.BlockSpec((tt, d), lambda i, j: (i, 0)),
                  pl.BlockSpec((1,) + mod.shape[1:], lambda i, j: (i // tpb, 0, 0)),
                  vec, vec,
                  pl.BlockSpec(wq.shape, lambda i, j: (0, 0), pipeline_mode=pl.Buffered(1)),
                  pl.BlockSpec(keys.shape, lambda i, j: (0, 0, 0), pipeline_mode=pl.Buffered(1)),
                  pl.BlockSpec((eb, d), lambda i, j: (j, 0)),
                  pl.BlockSpec((d, eb), lambda i, j: (0, j))],
        out_specs=pl.BlockSpec((tt, d), lambda i, j: (i, 0)),
        out_shape=jax.ShapeDtypeStruct((t, d), F32),
        scratch_shapes=[pltpu.VMEM((tt, d), BF16),
                        pltpu.VMEM((2 * PEER_HEADS, nk, tt), F32),
                        pltpu.VMEM((PEER_HEADS, nk, tt), F32),
                        pltpu.VMEM((PEER_HEADS * nk // SUBLANES, SUBLANES, tt), F32),
                        pltpu.VMEM((2, PEER_TOPK + 1, PEER_HEADS, tt), F32),
                        pltpu.VMEM((2, PEER_HEADS, tt), F32),
                        pltpu.VMEM((eb, tt), F32),
                        pltpu.VMEM((eb, tt), BF16),
                        pltpu.VMEM((d, tt), F32),
                        pltpu.VMEM((eb, tt), F32)],
        compiler_params=_params(("arbitrary", "arbitrary")),
        name="peer",
    )(h1, mod, g2, gf, wq, keys, u_tab, vt_tab)


def kernel(x, c, ada_w, ada_b, norm1_g, w_in, lb_logits, hg_norm_g, w_a, conv_w, conv_b,
           conv_ln_g, conv_ln_b, w_b, w_out, norm2_g, peer_wq, peer_keys, peer_u, peer_v,
           final_g):
    bsz, seq, d = x.shape
    depth = ada_w.shape[0]
    assert depth == 1 and lb_logits.shape[0] == depth + 1
    assert seq % HG_CHUNK == 0 and d % LANES == 0
    t = bsz * seq
    tm = min(512, seq)
    ts = min(256, seq)
    tt = min(512, seq)
    nk = peer_keys.shape[3]
    assert nk == LANES and peer_keys.shape[1] == PEER_HEADS
    eb = SUBLANES * nk
    l = 0

    x2 = x.reshape(t, d)
    ada = _ada(c, ada_w[l].astype(BF16), ada_b[l].reshape(1, -1))
    mod = ada.reshape(bsz, 6, d)

    q, f, v, sg, glu, ga, gb = _in_proj(
        x2, mod, norm1_g[l].reshape(1, d), lb_logits, w_in[l].astype(BF16), seq, min(256, seq))
    og = _hgrn2(q, f, v, sg, hg_norm_g[l].reshape(1, d), bsz, seq, HG_HEADS // 2,
                min(4, seq // HG_CHUNK))
    cv = _conv(glu, conv_w[l], conv_b[l].reshape(1, d), conv_ln_g[l].reshape(1, d),
               conv_ln_b[l].reshape(1, d), bsz, seq, ts, 8, 4)
    h1 = _merge(og, cv, ga, gb, x2, mod, w_a[l].astype(BF16), w_b[l].astype(BF16),
                w_out[l].astype(BF16), seq, tm)
    keys = peer_keys[l].reshape(2 * PEER_HEADS, nk, -1).astype(BF16)
    out = _peer(h1, mod, norm2_g[l].reshape(1, d), final_g.reshape(1, d),
                peer_wq[l].astype(BF16), keys, peer_u[l].astype(BF16),
                peer_v[l].T.astype(BF16), seq, tt, eb, 16)
    return out.reshape(bsz, seq, d)
```
